```python
import jax, jax.numpy as jnp
from jax import lax
import numpy as np

D_MODEL = 2048
BATCH = 1
SEQ = 16384
DEPTH = 2
DEC_BATCH = 2
DEC_SEQ = 8192
PAST_LEN = 128

RET_HEADS = 8
RET_DK = D_MODEL // RET_HEADS
RET_DV = 2 * D_MODEL // RET_HEADS
RET_CHUNK = 128
ROPE_BASE = 10000.0
SGU_E = 3 * D_MODEL
SGU_GROUPS = 8
SGU_CHUNK = 128
FFN_DIM = 5632
CONV_W = 3
N_RET = (DEPTH + 1) // 2
N_SGU = DEPTH // 2
EPS = 1e-6
LN_EPS = 1e-5

kernel_name = "bidir_retention_gmlp_convffn_hybrid"


def rms_norm(x, g):
    x32 = x.astype(jnp.float32)
    y = x32 * lax.rsqrt(jnp.mean(x32 * x32, axis=-1, keepdims=True) + EPS)
    return (y * g.astype(jnp.float32)).astype(x.dtype)


def rope(x):
    S, d = x.shape[1], x.shape[-1]
    half = d // 2
    inv = ROPE_BASE ** (-jnp.arange(half, dtype=jnp.float32) / half)
    ang = jnp.arange(S, dtype=jnp.float32)[:, None] * inv[None, :]
    cos = jnp.cos(ang)[None, :, None, :].astype(x.dtype)
    sin = jnp.sin(ang)[None, :, None, :].astype(x.dtype)
    x1, x2 = x[..., :half], x[..., half:]
    return jnp.concatenate([x1 * cos - x2 * sin, x2 * cos + x1 * sin], axis=-1)


def retention_one_direction(q, k, v, log_gamma, include_diag):
    B, S, H, dk = q.shape
    dv = v.shape[-1]
    n_chunks = S // RET_CHUNK

    def to_chunks(t):
        return t.reshape(B, n_chunks, RET_CHUNK, H, t.shape[-1]).transpose(1, 0, 3, 2, 4)

    qc, kc, vc = to_chunks(q), to_chunks(k), to_chunks(v)
    lg = log_gamma.astype(jnp.float32)
    pos = jnp.arange(RET_CHUNK, dtype=jnp.float32)
    diff = pos[:, None] - pos[None, :]
    mask = (diff >= 0) if include_diag else (diff > 0)
    decay_intra = jnp.where(mask, jnp.exp(lg[:, None, None] * jnp.maximum(diff, 0.0)), 0.0)
    q_decay = jnp.exp(lg[:, None] * (pos + 1.0))
    k_decay = jnp.exp(lg[:, None] * (RET_CHUNK - 1.0 - pos))
    chunk_decay = jnp.exp(lg * RET_CHUNK)

    def step(state, inp):
        qi, ki, vi = inp
        qi = qi.astype(jnp.float32)
        ki = ki.astype(jnp.float32)
        vi = vi.astype(jnp.float32)
        scores = jnp.einsum('bhid,bhjd->bhij', qi, ki) * decay_intra
        intra = jnp.einsum('bhij,bhjv->bhiv', scores, vi)
        cross = jnp.einsum('bhid,bhdv->bhiv', qi * q_decay[:, :, None], state)
        new_state = state * chunk_decay[:, None, None] + jnp.einsum(
            'bhjd,bhjv->bhdv', ki * k_decay[:, :, None], vi)
        return new_state, (intra + cross).astype(v.dtype)

    state0 = jnp.zeros((B, H, dk, dv), jnp.float32)
    _, out = lax.scan(step, state0, (qc, kc, vc))
    return out.transpose(1, 0, 3, 2, 4).reshape(B, S, H, dv)


def retention_mixer(h, w_in, log_decay_fwd, log_decay_bwd, gn_gain, w_out):
    B, S, _ = h.shape
    hq = RET_HEADS * RET_DK
    hv = RET_HEADS * RET_DV
    proj = h @ w_in
    q, k, v, g = jnp.split(proj, [hq, 2 * hq, 2 * hq + hv], axis=-1)
    q = rope(q.reshape(B, S, RET_HEADS, RET_DK))
    k = rope(k.reshape(B, S, RET_HEADS, RET_DK)) * (RET_DK ** -0.5)
    v = v.reshape(B, S, RET_HEADS, RET_DV)
    y_fwd = retention_one_direction(q, k, v, log_decay_fwd, True)
    flip = lambda t: jnp.flip(t, axis=1)
    y_bwd = flip(retention_one_direction(flip(q), flip(k), flip(v), log_decay_bwd, False))
    y = (y_fwd + y_bwd).astype(jnp.float32)
    mu = jnp.mean(y, axis=-1, keepdims=True)
    var = jnp.mean(jnp.square(y - mu), axis=-1, keepdims=True)
    y = ((y - mu) * lax.rsqrt(var + LN_EPS)).reshape(B, S, hv) * gn_gain.astype(jnp.float32)
    return (jax.nn.silu(g) * y.astype(h.dtype)) @ w_out


def sgu_mixer(h, w_in, b_in, ln_g, ln_b, w_s, b_s, w_out):
    B, S, _ = h.shape
    z = jax.nn.gelu(h @ w_in + b_in, approximate=False)
    u, v = jnp.split(z, 2, axis=-1)
    v32 = v.astype(jnp.float32)
    mu = jnp.mean(v32, axis=-1, keepdims=True)
    var = jnp.mean(jnp.square(v32 - mu), axis=-1, keepdims=True)
    vn = ((v32 - mu) * lax.rsqrt(var + LN_EPS) * ln_g.astype(jnp.float32)
          + ln_b.astype(jnp.float32)).astype(h.dtype)
    n_chunks = S // SGU_CHUNK
    vg = vn.reshape(B, n_chunks, SGU_CHUNK, SGU_GROUPS, SGU_E // SGU_GROUPS)
    mixed = jnp.einsum('gij,bnjgc->bnigc', w_s, vg) + b_s.T[:, :, None]
    return (u * mixed.reshape(B, S, SGU_E)) @ w_out


def conv_ffn(h, w_in, conv_w, conv_b, w_out):
    a = h @ w_in
    ap = jnp.pad(a, ((0, 0), (1, 1), (0, 0)))
    a = ap[:, :-2] * conv_w[0] + ap[:, 1:-1] * conv_w[1] + ap[:, 2:] * conv_w[2] + conv_b
    gate, up = jnp.split(a, 2, axis=-1)
    return (jax.nn.silu(gate) * up) @ w_out


def trunk(x, ln_mix, ln_ffn, ret_w_in, ret_log_decay_fwd, ret_log_decay_bwd, ret_gn_gain,
          ret_w_out, sgu_w_in, sgu_b_in, sgu_ln_g, sgu_ln_b, sgu_w_s, sgu_b_s, sgu_w_out,
          ffn_w_in, ffn_conv_w, ffn_conv_b, ffn_w_out, ln_final):
    for i in range(DEPTH):
        j = i // 2
        hn = rms_norm(x, ln_mix[i])
        if i % 2 == 0:
            x = x + retention_mixer(hn, ret_w_in[j], ret_log_decay_fwd[j], ret_log_decay_bwd[j],
                                    ret_gn_gain[j], ret_w_out[j])
        else:
            x = x + sgu_mixer(hn, sgu_w_in[j], sgu_b_in[j], sgu_ln_g[j], sgu_ln_b[j],
                              sgu_w_s[j], sgu_b_s[j], sgu_w_out[j])
        hn = rms_norm(x, ln_ffn[i])
        x = x + conv_ffn(hn, ffn_w_in[i], ffn_conv_w[i], ffn_conv_b[i], ffn_w_out[i])
    return rms_norm(x, ln_final)


def setup_inputs(seed: int = 0) -> dict:
    key = jax.random.key(seed)
    ks = jax.random.split(key, 24)
    f32 = jnp.float32
    nrm = lambda k, shape, s: jax.random.normal(k, shape, f32) * s
    hq = RET_HEADS * RET_DK
    hv = RET_HEADS * RET_DV
    base_decay = jnp.asarray(np.log(1.0 - 2.0 ** (-5.0 - np.arange(RET_HEADS))), f32)
    conv_center = jnp.asarray([0.0, 1.0, 0.0], f32)[None, :, None]
    return {
        "x_prompt": jax.random.normal(ks[0], (BATCH, SEQ, D_MODEL), f32),
        "x_sample": jax.random.normal(ks[1], (DEC_BATCH, DEC_SEQ, D_MODEL), f32),
        "ln_mix": 1.0 + nrm(ks[2], (DEPTH, D_MODEL), 0.02),
        "ln_ffn": 1.0 + nrm(ks[3], (DEPTH, D_MODEL), 0.02),
        "ret_w_in": nrm(ks[4], (N_RET, D_MODEL, 2 * hq + 2 * hv), D_MODEL ** -0.5),
        "ret_log_decay_fwd": base_decay[None, :] * jnp.exp(nrm(ks[5], (N_RET, RET_HEADS), 0.1)),
        "ret_log_decay_bwd": base_decay[None, :] * jnp.exp(nrm(ks[6], (N_RET, RET_HEADS), 0.1)),
        "ret_gn_gain": 1.0 + nrm(ks[7], (N_RET, hv), 0.02),
        "ret_w_out": nrm(ks[8], (N_RET, hv, D_MODEL), hv ** -0.5),
        "sgu_w_in": nrm(ks[9], (N_SGU, D_MODEL, 2 * SGU_E), D_MODEL ** -0.5),
        "sgu_b_in": nrm(ks[10], (N_SGU, 2 * SGU_E), 0.02),
        "sgu_ln_g": 1.0 + nrm(ks[11], (N_SGU, SGU_E), 0.02),
        "sgu_ln_b": nrm(ks[12], (N_SGU, SGU_E), 0.02),
        "sgu_w_s": nrm(ks[13], (N_SGU, SGU_GROUPS, SGU_CHUNK, SGU_CHUNK), SGU_CHUNK ** -0.5),
        "sgu_b_s": 1.0 + nrm(ks[14], (N_SGU, SGU_GROUPS, SGU_CHUNK), 0.02),
        "sgu_w_out": nrm(ks[15], (N_SGU, SGU_E, D_MODEL), SGU_E ** -0.5),
        "ffn_w_in": nrm(ks[16], (DEPTH, D_MODEL, 2 * FFN_DIM), D_MODEL ** -0.5),
        "ffn_conv_w": conv_center + nrm(ks[17], (DEPTH, CONV_W, 2 * FFN_DIM), 0.2),
        "ffn_conv_b": nrm(ks[18], (DEPTH, 2 * FFN_DIM), 0.02),
        "ffn_w_out": nrm(ks[19], (DEPTH, FFN_DIM, D_MODEL), FFN_DIM ** -0.5),
        "ln_final": 1.0 + nrm(ks[20], (D_MODEL,), 0.02),
    }


def reference(x_prompt, x_sample, ln_mix, ln_ffn, ret_w_in, ret_log_decay_fwd, ret_log_decay_bwd,
              ret_gn_gain, ret_w_out, sgu_w_in, sgu_b_in, sgu_ln_g, sgu_ln_b, sgu_w_s, sgu_b_s,
              sgu_w_out, ffn_w_in, ffn_conv_w, ffn_conv_b, ffn_w_out, ln_final):
    params = (ln_mix, ln_ffn, ret_w_in, ret_log_decay_fwd, ret_log_decay_bwd, ret_gn_gain,
              ret_w_out, sgu_w_in, sgu_b_in, sgu_ln_g, sgu_ln_b, sgu_w_s, sgu_b_s, sgu_w_out,
              ffn_w_in, ffn_conv_w, ffn_conv_b, ffn_w_out, ln_final)
    y_prompt = trunk(x_prompt, *params)
    y_sample = trunk(x_sample, *params)
    return (y_prompt, y_sample)
```

```python
import functools

import jax
import jax.numpy as jnp
import numpy as np
from jax import lax
from jax.experimental import pallas as pl
from jax.experimental.pallas import tpu as pltpu

D_MODEL = 2048
RET_HEADS = 8
RET_DK = D_MODEL // RET_HEADS
RET_DV = 2 * D_MODEL // RET_HEADS
ROPE_BASE = 10000.0
SGU_E = 3 * D_MODEL
SGU_GROUPS = 8
SGU_CHUNK = 128
SGU_GW = SGU_E // SGU_GROUPS
FFN_DIM = 5632
EPS = 1e-6
LN_EPS = 1e-5

LANES = 128
SUBLANES_F32 = 8
VMEM_LIMIT_BYTES = 56 * 1024 * 1024

PROJ_TM = 1024
PROJ_TN = 512
RET_BLOCK = 256
OUT_TM = 1024
OUT_TN = 512
SGU_TM = 512
FFN_TM = 512
FFN_TF = 512
HALO = SUBLANES_F32

BF16 = jnp.bfloat16
F32 = jnp.float32


def _params(*semantics):
    return pltpu.CompilerParams(dimension_semantics=semantics,
                                vmem_limit_bytes=VMEM_LIMIT_BYTES)


def _rms_rows(x, gain):
    ms = jnp.mean(x * x, axis=-1, keepdims=True)
    return x * lax.rsqrt(ms + EPS) * gain


def _dot(a, b):
    return jnp.dot(a, b, preferred_element_type=F32)


def _ret_proj_kernel(x_ref, gain_ref, w_ref, cos_ref, sin_ref, o_ref, hn_ref,
                     *, n_rope, n_plain):
    j = pl.program_id(2)

    @pl.when(j == 0)
    def _():
        hn_ref[...] = _rms_rows(x_ref[0], gain_ref[...]).astype(BF16)

    acc = _dot(hn_ref[...], w_ref[...])

    @pl.when(j < n_rope)
    def _():
        cos = cos_ref[...]
        sin = sin_ref[...]
        half = RET_DK // 2
        for hh in range(PROJ_TN // RET_DK):
            lo = hh * RET_DK
            x1 = acc[:, lo:lo + half]
            x2 = acc[:, lo + half:lo + RET_DK]
            o_ref[0, :, lo:lo + half] = (x1 * cos - x2 * sin).astype(BF16)
            o_ref[0, :, lo + half:lo + RET_DK] = (x2 * cos + x1 * sin).astype(BF16)

    @pl.when(jnp.logical_and(j >= n_rope, j < n_rope + n_plain))
    def _():
        o_ref[0] = acc.astype(BF16)

    @pl.when(j >= n_rope + n_plain)
    def _():
        o_ref[0] = (acc * jax.nn.sigmoid(acc)).astype(BF16)


def _ret_proj(x, gain, w_qvg, cos, sin):
    B, S, D = x.shape
    N = w_qvg.shape[1]
    tm, tn = PROJ_TM, PROJ_TN
    hq = RET_HEADS * RET_DK
    hv = RET_HEADS * RET_DV
    kern = functools.partial(_ret_proj_kernel, n_rope=hq // tn, n_plain=hv // tn)
    return pl.pallas_call(
        kern,
        grid=(B, S // tm, N // tn),
        in_specs=[
            pl.BlockSpec((1, tm, D), lambda b, i, j: (b, i, 0)),
            pl.BlockSpec((1, D), lambda b, i, j: (0, 0)),
            pl.BlockSpec((D, tn), lambda b, i, j: (0, j)),
            pl.BlockSpec((tm, RET_DK // 2), lambda b, i, j: (i, 0)),
            pl.BlockSpec((tm, RET_DK // 2), lambda b, i, j: (i, 0)),
        ],
        out_specs=pl.BlockSpec((1, tm, tn), lambda b, i, j: (b, i, j)),
        out_shape=jax.ShapeDtypeStruct((B, S, N), BF16),
        scratch_shapes=[pltpu.VMEM((tm, D), BF16)],
        compiler_params=_params("parallel", "parallel", "arbitrary"),
        name="ret_proj",
    )(x, gain, w_qvg, cos, sin)


def _ret_projT_kernel(x_ref, gain_ref, wt_ref, cos_ref, sin_ref, o_ref, hn_ref,
                      *, scale):
    j = pl.program_id(2)

    @pl.when(j == 0)
    def _():
        hn_ref[...] = _rms_rows(x_ref[0], gain_ref[...]).astype(BF16)

    acc = lax.dot_general(wt_ref[...], hn_ref[...], (((1,), (1,)), ((), ())),
                          preferred_element_type=F32)
    cos = cos_ref[...]
    sin = sin_ref[...]
    half = RET_DK // 2
    for hh in range(PROJ_TN // RET_DK):
        lo = hh * RET_DK
        x1 = acc[lo:lo + half, :]
        x2 = acc[lo + half:lo + RET_DK, :]
        o_ref[0, lo:lo + half, :] = ((x1 * cos - x2 * sin) * scale).astype(BF16)
        o_ref[0, lo + half:lo + RET_DK, :] = ((x2 * cos + x1 * sin) * scale).astype(BF16)


def _ret_projT(x, gain, w_kT, cosT, sinT):
    B, S, D = x.shape
    N = w_kT.shape[0]
    tm, tn = PROJ_TM, PROJ_TN
    kern = functools.partial(_ret_projT_kernel, scale=RET_DK ** -0.5)
    return pl.pallas_call(
        kern,
        grid=(B, S // tm, N // tn),
        in_specs=[
            pl.BlockSpec((1, tm, D), lambda b, i, j: (b, i, 0)),
            pl.BlockSpec((1, D), lambda b, i, j: (0, 0)),
            pl.BlockSpec((tn, D), lambda b, i, j: (j, 0)),
            pl.BlockSpec((RET_DK // 2, tm), lambda b, i, j: (0, i)),
            pl.BlockSpec((RET_DK // 2, tm), lambda b, i, j: (0, i)),
        ],
        out_specs=pl.BlockSpec((1, tn, tm), lambda b, i, j: (b, j, i)),
        out_shape=jax.ShapeDtypeStruct((B, N, S), BF16),
        scratch_shapes=[pltpu.VMEM((tm, D), BF16)],
        compiler_params=_params("parallel", "parallel", "arbitrary"),
        name="ret_projT",
    )(x, gain, w_kT, cosT, sinT)


def _retention_kernel(lg_ref, q_ref, kt_ref, v_ref, *rest, reverse, finish):
    if finish:
        yb_ref, gate_ref, gain_ref, o_ref = rest[:4]
        state_ref, dec_ref, qd_ref, kd_ref, cd_ref = rest[4:]
    else:
        o_ref = rest[0]
        state_ref, dec_ref, qd_ref, kd_ref, cd_ref = rest[1:]
    C = RET_BLOCK
    h = pl.program_id(1)
    t = pl.program_id(2)

    @pl.when(t == 0)
    def _():
        lg = lg_ref[h]
        state_ref[...] = jnp.zeros_like(state_ref)
        ri = lax.broadcasted_iota(jnp.int32, (C, C), 0)
        ci = lax.broadcasted_iota(jnp.int32, (C, C), 1)
        d = (ci - ri) if reverse else (ri - ci)
        mask = (d > 0) if reverse else (d >= 0)
        dist = jnp.maximum(d, 0).astype(F32)
        dec_ref[...] = jnp.where(mask, jnp.exp(lg * dist), 0.0)
        row = lax.broadcasted_iota(jnp.int32, (C, 1), 0).astype(F32)
        col = lax.broadcasted_iota(jnp.int32, (1, C), 1).astype(F32)
        if reverse:
            qd_ref[...] = jnp.exp(lg * (C - row))
            kd_ref[...] = jnp.exp(lg * col)
        else:
            qd_ref[...] = jnp.exp(lg * (row + 1.0))
            kd_ref[...] = jnp.exp(lg * (C - 1.0 - col))
        cd_ref[...] = jnp.exp(jnp.full(cd_ref.shape, lg * C, F32))

    q = q_ref[0]
    kt = kt_ref[0]
    v = v_ref[0]
    scores = _dot(q, kt) * dec_ref[...]
    y = _dot(scores.astype(BF16), v)
    y = y + qd_ref[...] * _dot(q, state_ref[...].astype(BF16))
    kts = (kt.astype(F32) * kd_ref[...]).astype(BF16)
    state_ref[...] = state_ref[...] * cd_ref[...] + _dot(kts, v)

    if finish:
        y = y + yb_ref[0]
        mu = jnp.mean(y, axis=-1, keepdims=True)
        yc = y - mu
        var = jnp.mean(yc * yc, axis=-1, keepdims=True)
        yn = yc * lax.rsqrt(var + LN_EPS) * gain_ref[...]
        o_ref[0] = (gate_ref[0].astype(F32) * yn).astype(BF16)
    else:
        o_ref[0] = y


def _retention(qvg, kT, log_gamma, *, reverse, y_other=None, gn_gain=None):
    B, S, _ = qvg.shape
    C = RET_BLOCK
    H = RET_HEADS
    nC = S // C
    finish = y_other is not None
    v_off = (H * RET_DK) // RET_DV
    g_off = v_off + H
    if reverse:
        cidx = lambda t: nC - 1 - t
    else:
        cidx = lambda t: t
    in_specs = [
        pl.BlockSpec(memory_space=pltpu.SMEM),
        pl.BlockSpec((1, C, RET_DK), lambda b, h, t: (b, cidx(t), h)),
        pl.BlockSpec((1, RET_DK, C), lambda b, h, t: (b, h, cidx(t))),
        pl.BlockSpec((1, C, RET_DV), lambda b, h, t: (b, cidx(t), v_off + h)),
    ]
    args = [log_gamma, qvg, kT, qvg]
    if finish:
        in_specs += [
            pl.BlockSpec((1, C, RET_DV), lambda b, h, t: (b, cidx(t), h)),
            pl.BlockSpec((1, C, RET_DV), lambda b, h, t: (b, cidx(t), g_off + h)),
            pl.BlockSpec((1, RET_DV), lambda b, h, t: (0, h)),
        ]
        args += [y_other, qvg, gn_gain]
    kern = functools.partial(_retention_kernel, reverse=reverse, finish=finish)
    return pl.pallas_call(
        kern,
        grid=(B, H, nC),
        in_specs=in_specs,
        out_specs=pl.BlockSpec((1, C, RET_DV), lambda b, h, t: (b, cidx(t), h)),
        out_shape=jax.ShapeDtypeStruct((B, S, H * RET_DV), BF16 if finish else F32),
        scratch_shapes=[
            pltpu.VMEM((RET_DK, RET_DV), F32),
            pltpu.VMEM((C, C), F32),
            pltpu.VMEM((C, 1), F32),
            pltpu.VMEM((1, C), F32),
            pltpu.VMEM((1, RET_DV), F32),
        ],
        compiler_params=_params("parallel", "parallel", "arbitrary"),
        name="retention_rev" if reverse else "retention_fwd",
    )(*args)


def _out_proj_kernel(a_ref, w_ref, x_ref, o_ref):
    o_ref[0] = x_ref[0] + _dot(a_ref[0], w_ref[...])


def _out_proj(a, w, x):
    B, S, K = a.shape
    N = w.shape[1]
    tm, tn = OUT_TM, OUT_TN
    return pl.pallas_call(
        _out_proj_kernel,
        grid=(B, S // tm, N // tn),
        in_specs=[
            pl.BlockSpec((1, tm, K), lambda b, i, j: (b, i, 0)),
            pl.BlockSpec((K, tn), lambda b, i, j: (0, j)),
            pl.BlockSpec((1, tm, tn), lambda b, i, j: (b, i, j)),
        ],
        out_specs=pl.BlockSpec((1, tm, tn), lambda b, i, j: (b, i, j)),
        out_shape=jax.ShapeDtypeStruct((B, S, N), F32),
        compiler_params=_params("parallel", "parallel", "arbitrary"),
        name="out_proj",
    )(a, w, x)


def _sgu_in_kernel(x_ref, gain_ref, w_ref, b_ref, z_ref, mean_ref, rstd_ref,
                   hn_ref, s1_ref, s2_ref, *, n_u, n_tiles):
    j = pl.program_id(2)

    @pl.when(j == 0)
    def _():
        hn_ref[...] = _rms_rows(x_ref[0], gain_ref[...]).astype(BF16)
        s1_ref[...] = jnp.zeros_like(s1_ref)
        s2_ref[...] = jnp.zeros_like(s2_ref)

    acc = _dot(hn_ref[...], w_ref[...]) + b_ref[...]
    z = 0.5 * acc * (1.0 + lax.erf(acc * np.float32(np.sqrt(0.5))))
    z_ref[0] = z.astype(BF16)

    @pl.when(j >= n_u)
    def _():
        s1_ref[...] += jnp.sum(z, axis=-1, keepdims=True)
        s2_ref[...] += jnp.sum(z * z, axis=-1, keepdims=True)

    @pl.when(j == n_tiles - 1)
    def _():
        mean = s1_ref[...] * (1.0 / SGU_E)
        var = s2_ref[...] * (1.0 / SGU_E) - mean * mean
        mean_ref[0] = mean
        rstd_ref[0] = lax.rsqrt(var + LN_EPS)


def _sgu_in(x, gain, w, b):
    B, S, D = x.shape
    N = w.shape[1]
    tm, tn = PROJ_TM, PROJ_TN
    n_tiles = N // tn
    kern = functools.partial(_sgu_in_kernel, n_u=SGU_E // tn, n_tiles=n_tiles)
    return pl.pallas_call(
        kern,
        grid=(B, S // tm, n_tiles),
        in_specs=[
            pl.BlockSpec((1, tm, D), lambda b_, i, j: (b_, i, 0)),
            pl.BlockSpec((1, D), lambda b_, i, j: (0, 0)),
            pl.BlockSpec((D, tn), lambda b_, i, j: (0, j)),
            pl.BlockSpec((1, tn), lambda b_, i, j: (0, j)),
        ],
        out_specs=[
            pl.BlockSpec((1, tm, tn), lambda b_, i, j: (b_, i, j)),
            pl.BlockSpec((1, tm, 1), lambda b_, i, j: (b_, i, 0)),
            pl.BlockSpec((1, tm, 1), lambda b_, i, j: (b_, i, 0)),
        ],
        out_shape=[
            jax.ShapeDtypeStruct((B, S, N), BF16),
            jax.ShapeDtypeStruct((B, S, 1), F32),
            jax.ShapeDtypeStruct((B, S, 1), F32),
        ],
        scratch_shapes=[
            pltpu.VMEM((tm, D), BF16),
            pltpu.VMEM((tm, 1), F32),
            pltpu.VMEM((tm, 1), F32),
        ],
        compiler_params=_params("parallel", "parallel", "arbitrary"),
        name="sgu_in",
    )(x, gain, w, b)


def _sgu_out_kernel(u_ref, v_ref, mean_ref, rstd_ref, lng_ref, lnb_ref, ws_ref,
                    bs_ref, wo_ref, x_ref, o_ref, acc_ref, gated_ref):
    g = pl.program_id(2)

    @pl.when(g == 0)
    def _():
        acc_ref[...] = jnp.zeros_like(acc_ref)

    vn = ((v_ref[0].astype(F32) - mean_ref[0]) * rstd_ref[0] * lng_ref[...]
          + lnb_ref[...]).astype(BF16)
    ws = ws_ref[0]
    bs = bs_ref[0]
    for c in range(SGU_TM // SGU_CHUNK):
        rows = slice(c * SGU_CHUNK, (c + 1) * SGU_CHUNK)
        mixed = _dot(ws, vn[rows]) + bs
        gated_ref[rows, :] = (u_ref[0, rows, :].astype(F32) * mixed).astype(BF16)
    acc_ref[...] += _dot(gated_ref[...], wo_ref[...])

    @pl.when(g == SGU_GROUPS - 1)
    def _():
        o_ref[0] = x_ref[0] + acc_ref[...]


def _sgu_out(z, mean, rstd, ln_g, ln_b, w_s, b_s, w_out, x):
    B, S, D = x.shape
    tm = SGU_TM
    G = SGU_GROUPS
    gw = SGU_GW
    return pl.pallas_call(
        _sgu_out_kernel,
        grid=(B, S // tm, G),
        in_specs=[
            pl.BlockSpec((1, tm, gw), lambda b, i, g: (b, i, g)),
            pl.BlockSpec((1, tm, gw), lambda b, i, g: (b, i, G + g)),
            pl.BlockSpec((1, tm, 1), lambda b, i, g: (b, i, 0)),
            pl.BlockSpec((1, tm, 1), lambda b, i, g: (b, i, 0)),
            pl.BlockSpec((1, gw), lambda b, i, g: (0, g)),
            pl.BlockSpec((1, gw), lambda b, i, g: (0, g)),
            pl.BlockSpec((1, SGU_CHUNK, SGU_CHUNK), lambda b, i, g: (g, 0, 0)),
            pl.BlockSpec((1, SGU_CHUNK, 1), lambda b, i, g: (g, 0, 0)),
            pl.BlockSpec((gw, D), lambda b, i, g: (g, 0)),
            pl.BlockSpec((1, tm, D), lambda b, i, g: (b, i, 0)),
        ],
        out_specs=pl.BlockSpec((1, tm, D), lambda b, i, g: (b, i, 0)),
        out_shape=jax.ShapeDtypeStruct((B, S, D), F32),
        scratch_shapes=[
            pltpu.VMEM((tm, D), F32),
            pltpu.VMEM((tm, gw), BF16),
        ],
        compiler_params=_params("parallel", "parallel", "arbitrary"),
        name="sgu_out",
    )(z, z, mean, rstd, ln_g, ln_b, w_s, b_s, w_out, x)


def _conv3(a, cw_ref, cb_ref, tm):
    ext = a.shape[0]
    prev = pltpu.roll(a, 1, axis=0)[:tm]
    nxt = pltpu.roll(a, ext - 1, axis=0)[:tm]
    return (prev * cw_ref[0:1, :] + a[:tm] * cw_ref[1:2, :]
            + nxt * cw_ref[2:3, :] + cb_ref[...])


def _conv_ffn_kernel(x_ref, xprev_ref, xnext_ref, gain_ref, wg_ref, wu_ref,
                     cwg_ref, cwu_ref, cbg_ref, cbu_ref, wo_ref, fin_ref, o_ref,
                     hn_ref, acc_ref, *, n_row_tiles, n_f, final_norm):
    i = pl.program_id(1)
    j = pl.program_id(2)
    tm = FFN_TM

    @pl.when(j == 0)
    def _():
        gain = gain_ref[...]
        hn_ref[0:tm, :] = _rms_rows(x_ref[0], gain).astype(BF16)
        has_next = (i < n_row_tiles - 1).astype(F32)
        has_prev = (i > 0).astype(F32)
        halo = jnp.concatenate([_rms_rows(xnext_ref[0], gain) * has_next,
                                _rms_rows(xprev_ref[0], gain) * has_prev], axis=0)
        hn_ref[tm:tm + 2 * HALO, :] = halo.astype(BF16)
        acc_ref[...] = jnp.zeros_like(acc_ref)

    hn = hn_ref[...]
    gate = _conv3(_dot(hn, wg_ref[...]), cwg_ref, cbg_ref, tm)
    up = _conv3(_dot(hn, wu_ref[...]), cwu_ref, cbu_ref, tm)
    act = (gate * jax.nn.sigmoid(gate) * up).astype(BF16)
    acc_ref[...] += _dot(act, wo_ref[...])

    @pl.when(j == n_f - 1)
    def _():
        y = x_ref[0] + acc_ref[...]
        if final_norm:
            y = _rms_rows(y, fin_ref[...])
        o_ref[0] = y


def _conv_ffn(x, gain, w_in, conv_w, conv_b, w_out, fin_gain, *, final_norm):
    B, S, D = x.shape
    tm, tf = FFN_TM, FFN_TF
    F = FFN_DIM
    n_f = F // tf
    n_row_tiles = S // tm
    hb = tm // HALO
    last_hb = S // HALO - 1
    kern = functools.partial(_conv_ffn_kernel, n_row_tiles=n_row_tiles, n_f=n_f,
                             final_norm=final_norm)
    return pl.pallas_call(
        kern,
        grid=(B, n_row_tiles, n_f),
        in_specs=[
            pl.BlockSpec((1, tm, D), lambda b, i, j: (b, i, 0)),
            pl.BlockSpec((1, HALO, D),
                         lambda b, i, j: (b, jnp.maximum(i * hb - 1, 0), 0)),
            pl.BlockSpec((1, HALO, D),
                         lambda b, i, j: (b, jnp.minimum((i + 1) * hb, last_hb), 0)),
            pl.BlockSpec((1, D), lambda b, i, j: (0, 0)),
            pl.BlockSpec((D, tf), lambda b, i, j: (0, j)),
            pl.BlockSpec((D, tf), lambda b, i, j: (0, n_f + j)),
            pl.BlockSpec((3, tf), lambda b, i, j: (0, j)),
            pl.BlockSpec((3, tf), lambda b, i, j: (0, n_f + j)),
            pl.BlockSpec((1, tf), lambda b, i, j: (0, j)),
            pl.BlockSpec((1, tf), lambda b, i, j: (0, n_f + j)),
            pl.BlockSpec((tf, D), lambda b, i, j: (j, 0)),
            pl.BlockSpec((1, D), lambda b, i, j: (0, 0)),
        ],
        out_specs=pl.BlockSpec((1, tm, D), lambda b, i, j: (b, i, 0)),
        out_shape=jax.ShapeDtypeStruct((B, S, D), F32),
        scratch_shapes=[
            pltpu.VMEM((tm + 2 * HALO, D), BF16),
            pltpu.VMEM((tm, D), F32),
        ],
        compiler_params=_params("parallel", "parallel", "arbitrary"),
        name="conv_ffn",
    )(x, x, x, gain, w_in, w_in, conv_w, conv_w, conv_b, conv_b, w_out, fin_gain)


def _rope_tables(S):
    half = RET_DK // 2
    inv = ROPE_BASE ** (-jnp.arange(half, dtype=F32) / half)
    ang = jnp.arange(S, dtype=F32)[:, None] * inv[None, :]
    return jnp.cos(ang), jnp.sin(ang)


def _row(v):
    return v.reshape(1, -1)


def _prepare(ln_mix, ln_ffn, ret_w_in, ret_log_decay_fwd, ret_log_decay_bwd,
             ret_gn_gain, ret_w_out, sgu_w_in, sgu_b_in, sgu_ln_g, sgu_ln_b,
             sgu_w_s, sgu_b_s, sgu_w_out, ffn_w_in, ffn_conv_w, ffn_conv_b,
             ffn_w_out, ln_final):
    depth = ln_mix.shape[0]
    hq = RET_HEADS * RET_DK
    layers = []
    for i in range(depth):
        j = i // 2
        p = {"ln_mix": _row(ln_mix[i]), "ln_ffn": _row(ln_ffn[i])}
        if i % 2 == 0:
            w = ret_w_in[j]
            p["w_qvg"] = jnp.concatenate([w[:, :hq], w[:, 2 * hq:]], axis=1).astype(BF16)
            p["w_kT"] = w[:, hq:2 * hq].T.astype(BF16)
            p["lg_fwd"] = ret_log_decay_fwd[j]
            p["lg_bwd"] = ret_log_decay_bwd[j]
            p["gn_gain"] = _row(ret_gn_gain[j])
            p["w_out"] = ret_w_out[j].astype(BF16)
        else:
            p["w_in"] = sgu_w_in[j].astype(BF16)
            p["b_in"] = _row(sgu_b_in[j])
            p["ln_g"] = _row(sgu_ln_g[j])
            p["ln_b"] = _row(sgu_ln_b[j])
            p["w_s"] = sgu_w_s[j].astype(BF16)
            p["b_s"] = sgu_b_s[j][:, :, None]
            p["w_out"] = sgu_w_out[j].astype(BF16)
        p["ffn_w_in"] = ffn_w_in[i].astype(BF16)
        p["ffn_conv_w"] = ffn_conv_w[i]
        p["ffn_conv_b"] = _row(ffn_conv_b[i])
        p["ffn_w_out"] = ffn_w_out[i].astype(BF16)
        layers.append(p)
    return layers, _row(ln_final)


def _trunk(x, layers, fin_gain):
    S = x.shape[1]
    cos, sin = _rope_tables(S)
    cosT, sinT = cos.T, sin.T
    for i, p in enumerate(layers):
        if i % 2 == 0:
            qvg = _ret_proj(x, p["ln_mix"], p["w_qvg"], cos, sin)
            kT = _ret_projT(x, p["ln_mix"], p["w_kT"], cosT, sinT)
            y_rev = _retention(qvg, kT, p["lg_bwd"], reverse=True)
            a = _retention(qvg, kT, p["lg_fwd"], reverse=False, y_other=y_rev,
                           gn_gain=p["gn_gain"])
            x = _out_proj(a, p["w_out"], x)
        else:
            z, mean, rstd = _sgu_in(x, p["ln_mix"], p["w_in"], p["b_in"])
            x = _sgu_out(z, mean, rstd, p["ln_g"], p["ln_b"], p["w_s"], p["b_s"],
                         p["w_out"], x)
        x = _conv_ffn(x, p["ln_ffn"], p["ffn_w_in"], p["ffn_conv_w"],
                      p["ffn_conv_b"], p["ffn_w_out"], fin_gain,
                      final_norm=(i == len(layers) - 1))
    return x


def kernel(x_prompt, x_sample, ln_mix, ln_ffn, ret_w_in, ret_log_decay_fwd, ret_log_decay_bwd, ret_gn_gain, ret_w_out, sgu_w_in, sgu_b_in, sgu_ln_g, sgu_ln_b, sgu_w_s, sgu_b_s, sgu_w_out, ffn_w_in, ffn_conv_w, ffn_conv_b, ffn_w_out, ln_final):
    layers, fin_gain = _prepare(
        ln_mix, ln_ffn, ret_w_in, ret_log_decay_fwd, ret_log_decay_bwd, ret_gn_gain,
        ret_w_out, sgu_w_in, sgu_b_in, sgu_ln_g, sgu_ln_b, sgu_w_s, sgu_b_s,
        sgu_w_out, ffn_w_in, ffn_conv_w, ffn_conv_b, ffn_w_out, ln_final)
    return (_trunk(x_prompt, layers, fin_gain), _trunk(x_sample, layers, fin_gain))
```

```python
import functools

import jax
import jax.numpy as jnp
import numpy as np
from jax import lax
from jax.experimental import pallas as pl
from jax.experimental.pallas import tpu as pltpu

D_MODEL = 2048
RET_HEADS = 8
RET_DK = D_MODEL // RET_HEADS
RET_DV = 2 * D_MODEL // RET_HEADS
RET_HQ = RET_HEADS * RET_DK
RET_HV = RET_HEADS * RET_DV
ROPE_BASE = 10000.0
SGU_E = 3 * D_MODEL
SGU_GROUPS = 8
SGU_CHUNK = 128
SGU_GW = SGU_E // SGU_GROUPS
FFN_DIM = 5632
EPS = 1e-6
LN_EPS = 1e-5

SUBLANES_F32 = 8
LANES = 128
VMEM_LIMIT_BYTES = 56 * 1024 * 1024

PROJ_TM = 1024
PROJ_TN = 1024
PROJ_SUB = 512
KT_TN = 512
RET_BLOCK = 256
OUT_TM = 512
SGU_TM = 512
FFN_TM = 512
FFN_TF = 512
FFN_HALF = FFN_TF // 2
HALO = SUBLANES_F32

BF16 = jnp.bfloat16
F32 = jnp.float32


def _params(*semantics):
    return pltpu.CompilerParams(dimension_semantics=semantics,
                                vmem_limit_bytes=VMEM_LIMIT_BYTES)


def _rms_rows(x, gain):
    ms = jnp.mean(x * x, axis=-1, keepdims=True)
    return x * lax.rsqrt(ms + EPS) * gain


def _dot(a, b):
    return jnp.dot(a, b, preferred_element_type=F32)


def _silu(a):
    return a * jax.nn.sigmoid(a)


def _ret_proj_kernel(x_ref, gain_ref, w_ref, cos_ref, sin_ref, o_ref, hn_ref,
                     *, n_plain, n_silu):
    j = pl.program_id(2)

    @pl.when(j == 0)
    def _():
        hn_ref[...] = _rms_rows(x_ref[0], gain_ref[...]).astype(BF16)

    def store_plain(acc, lo):
        o_ref[0, :, lo:lo + PROJ_SUB] = acc.astype(BF16)

    def store_silu(acc, lo):
        o_ref[0, :, lo:lo + PROJ_SUB] = _silu(acc).astype(BF16)

    def store_rope(acc, lo):
        cos = cos_ref[...]
        sin = sin_ref[...]
        half = RET_DK // 2
        for hh in range(PROJ_SUB // RET_DK):
            a = hh * RET_DK
            x1 = acc[:, a:a + half]
            x2 = acc[:, a + half:a + RET_DK]
            o_ref[0, :, lo + a:lo + a + half] = (x1 * cos - x2 * sin).astype(BF16)
            o_ref[0, :, lo + a + half:lo + a + RET_DK] = (x2 * cos + x1 * sin).astype(BF16)

    def chains(store):
        for s in range(PROJ_TN // PROJ_SUB):
            lo = s * PROJ_SUB
            store(_dot(hn_ref[...], w_ref[:, lo:lo + PROJ_SUB]), lo)

    @pl.when(j < n_plain)
    def _():
        chains(store_plain)

    @pl.when(jnp.logical_and(j >= n_plain, j < n_plain + n_silu))
    def _():
        chains(store_silu)

    @pl.when(j >= n_plain + n_silu)
    def _():
        chains(store_rope)


def _ret_proj(x, gain, w_vgq, cos, sin):
    B, S, D = x.shape
    N = w_vgq.shape[1]
    tm, tn = PROJ_TM, PROJ_TN
    kern = functools.partial(_ret_proj_kernel, n_plain=RET_HV // tn, n_silu=RET_HV // tn)
    return pl.pallas_call(
        kern,
        grid=(B, S // tm, N // tn),
        in_specs=[
            pl.BlockSpec((1, tm, D), lambda b, i, j: (b, i, 0)),
            pl.BlockSpec((1, D), lambda b, i, j: (0, 0)),
            pl.BlockSpec((D, tn), lambda b, i, j: (0, j)),
            pl.BlockSpec((tm, RET_DK // 2), lambda b, i, j: (i, 0)),
            pl.BlockSpec((tm, RET_DK // 2), lambda b, i, j: (i, 0)),
        ],
        out_specs=pl.BlockSpec((1, tm, tn), lambda b, i, j: (b, i, j)),
        out_shape=jax.ShapeDtypeStruct((B, S, N), BF16),
        scratch_shapes=[pltpu.VMEM((tm, D), BF16)],
        compiler_params=_params("parallel", "parallel", "arbitrary"),
        name="ret_proj",
    )(x, gain, w_vgq, cos, sin)


def _ret_projT_kernel(x_ref, gain_ref, wt_ref, cos_ref, sin_ref, o_ref, hn_ref,
                      *, scale):
    j = pl.program_id(2)
    C = RET_BLOCK

    @pl.when(j == 0)
    def _():
        hn_ref[...] = _rms_rows(x_ref[0], gain_ref[...]).astype(BF16)

    acc = lax.dot_general(wt_ref[...], hn_ref[...], (((1,), (1,)), ((), ())),
                          preferred_element_type=F32)
    cos = cos_ref[...]
    sin = sin_ref[...]
    half = RET_DK // 2
    for hh in range(KT_TN // RET_DK):
        lo = hh * RET_DK
        x1 = acc[lo:lo + half, :]
        x2 = acc[lo + half:lo + RET_DK, :]
        r1 = ((x1 * cos - x2 * sin) * scale).astype(BF16)
        r2 = ((x2 * cos + x1 * sin) * scale).astype(BF16)
        for c in range(PROJ_TM // C):
            o_ref[0, c, lo:lo + half, :] = r1[:, c * C:(c + 1) * C]
            o_ref[0, c, lo + half:lo + RET_DK, :] = r2[:, c * C:(c + 1) * C]


def _ret_projT(x, gain, w_kT, cosT, sinT):
    B, S, D = x.shape
    N = w_kT.shape[0]
    tm, tn, C = PROJ_TM, KT_TN, RET_BLOCK
    kern = functools.partial(_ret_projT_kernel, scale=RET_DK ** -0.5)
    return pl.pallas_call(
        kern,
        grid=(B, S // tm, N // tn),
        in_specs=[
            pl.BlockSpec((1, tm, D), lambda b, i, j: (b, i, 0)),
            pl.BlockSpec((1, D), lambda b, i, j: (0, 0)),
            pl.BlockSpec((tn, D), lambda b, i, j: (j, 0)),
            pl.BlockSpec((RET_DK // 2, tm), lambda b, i, j: (0, i)),
            pl.BlockSpec((RET_DK // 2, tm), lambda b, i, j: (0, i)),
        ],
        out_specs=pl.BlockSpec((1, tm // C, tn, C), lambda b, i, j: (b, i, j, 0)),
        out_shape=jax.ShapeDtypeStruct((B, S // C, N, C), BF16),
        scratch_shapes=[pltpu.VMEM((tm, D), BF16)],
        compiler_params=_params("parallel", "parallel", "arbitrary"),
        name="ret_projT",
    )(x, gain, w_kT, cosT, sinT)


def _retention_kernel(lg_ref, q_ref, kt_ref, v_ref, *rest, reverse, finish):
    if finish:
        yb_ref, gate_ref, gain_ref, o_ref = rest[:4]
        state_ref, dec_ref, qd_ref, kd_ref, cd_ref = rest[4:]
    else:
        o_ref = rest[0]
        state_ref, dec_ref, qd_ref, kd_ref, cd_ref = rest[1:]
    C = RET_BLOCK
    t = pl.program_id(1)

    @pl.when(t == 0)
    def _():
        state_ref[...] = jnp.zeros_like(state_ref)
        ri = lax.broadcasted_iota(jnp.int32, (C, C), 0)
        ci = lax.broadcasted_iota(jnp.int32, (C, C), 1)
        d = (ci - ri) if reverse else (ri - ci)
        mask = (d > 0) if reverse else (d >= 0)
        dist = jnp.maximum(d, 0).astype(F32)
        row = lax.broadcasted_iota(jnp.int32, (C, 1), 0).astype(F32)
        col = lax.broadcasted_iota(jnp.int32, (1, C), 1).astype(F32)
        for h in range(RET_HEADS):
            lg = lg_ref[h]
            dec_ref[h] = jnp.where(mask, jnp.exp(lg * dist), 0.0)
            if reverse:
                qd_ref[h] = jnp.exp(lg * (C - row))
                kd_ref[h] = jnp.exp(lg * col)
            else:
                qd_ref[h] = jnp.exp(lg * (row + 1.0))
                kd_ref[h] = jnp.exp(lg * (C - 1.0 - col))
            cd_ref[h] = jnp.exp(jnp.full((1, RET_DV), lg * C, F32))

    for h in range(RET_HEADS):
        kcols = slice(h * RET_DK, (h + 1) * RET_DK)
        vcols = slice(h * RET_DV, (h + 1) * RET_DV)
        q = q_ref[0, :, kcols]
        kt = kt_ref[0, 0, kcols, :]
        v = v_ref[0, :, vcols]
        scores = _dot(q, kt) * dec_ref[h]
        y = _dot(scores.astype(BF16), v)
        y = y + qd_ref[h] * _dot(q, state_ref[h].astype(BF16))
        kts = (kt.astype(F32) * kd_ref[h]).astype(BF16)
        state_ref[h] = state_ref[h] * cd_ref[h] + _dot(kts, v)
        if finish:
            y = y + yb_ref[0, :, vcols].astype(F32)
            mu = jnp.mean(y, axis=-1, keepdims=True)
            yc = y - mu
            var = jnp.mean(yc * yc, axis=-1, keepdims=True)
            yn = yc * lax.rsqrt(var + LN_EPS) * gain_ref[:, vcols]
            o_ref[0, :, vcols] = (gate_ref[0, :, vcols].astype(F32) * yn).astype(BF16)
        else:
            o_ref[0, :, vcols] = y.astype(BF16)


def _retention(vgq, kT, log_gamma, *, reverse, y_other=None, gn_gain=None):
    B, S, _ = vgq.shape
    C = RET_BLOCK
    H = RET_HEADS
    nC = S // C
    finish = y_other is not None
    q_blk = 2 * RET_HV // RET_HQ
    if reverse:
        cidx = lambda t: nC - 1 - t
    else:
        cidx = lambda t: t
    in_specs = [
        pl.BlockSpec(memory_space=pltpu.SMEM),
        pl.BlockSpec((1, C, RET_HQ), lambda b, t: (b, cidx(t), q_blk)),
        pl.BlockSpec((1, 1, RET_HQ, C), lambda b, t: (b, cidx(t), 0, 0)),
        pl.BlockSpec((1, C, RET_HV), lambda b, t: (b, cidx(t), 0)),
    ]
    args = [log_gamma, vgq, kT, vgq]
    if finish:
        in_specs += [
            pl.BlockSpec((1, C, RET_HV), lambda b, t: (b, cidx(t), 0)),
            pl.BlockSpec((1, C, RET_HV), lambda b, t: (b, cidx(t), 1)),
            pl.BlockSpec((1, RET_HV), lambda b, t: (0, 0)),
        ]
        args += [y_other, vgq, gn_gain]
    kern = functools.partial(_retention_kernel, reverse=reverse, finish=finish)
    return pl.pallas_call(
        kern,
        grid=(B, nC),
        in_specs=in_specs,
        out_specs=pl.BlockSpec((1, C, RET_HV), lambda b, t: (b, cidx(t), 0)),
        out_shape=jax.ShapeDtypeStruct((B, S, RET_HV), BF16),
        scratch_shapes=[
            pltpu.VMEM((H, RET_DK, RET_DV), F32),
            pltpu.VMEM((H, C, C), F32),
            pltpu.VMEM((H, C, 1), F32),
            pltpu.VMEM((H, 1, C), F32),
            pltpu.VMEM((H, 1, RET_DV), F32),
        ],
        compiler_params=_params("parallel", "arbitrary"),
        name="retention_rev" if reverse else "retention_fwd",
    )(*args)


def _out_proj_kernel(a_ref, w_ref, x_ref, o_ref):
    N = w_ref.shape[1]
    for s in range(N // PROJ_SUB):
        cols = slice(s * PROJ_SUB, (s + 1) * PROJ_SUB)
        o_ref[0, :, cols] = x_ref[0, :, cols] + _dot(a_ref[0], w_ref[:, cols])


def _out_proj(a, w, x):
    B, S, K = a.shape
    N = w.shape[1]
    tm = OUT_TM
    return pl.pallas_call(
        _out_proj_kernel,
        grid=(B, S // tm),
        in_specs=[
            pl.BlockSpec((1, tm, K), lambda b, i: (b, i, 0)),
            pl.BlockSpec((K, N), lambda b, i: (0, 0), pipeline_mode=pl.Buffered(1)),
            pl.BlockSpec((1, tm, N), lambda b, i: (b, i, 0)),
        ],
        out_specs=pl.BlockSpec((1, tm, N), lambda b, i: (b, i, 0)),
        out_shape=jax.ShapeDtypeStruct((B, S, N), F32),
        compiler_params=_params("parallel", "parallel"),
        name="out_proj",
    )(a, w, x)


def _sgu_in_kernel(x_ref, gain_ref, w_ref, b_ref, z_ref, mean_ref, rstd_ref,
                   hn_ref, s1_ref, s2_ref, *, n_u, n_tiles):
    j = pl.program_id(2)

    @pl.when(j == 0)
    def _():
        hn_ref[...] = _rms_rows(x_ref[0], gain_ref[...]).astype(BF16)
        s1_ref[...] = jnp.zeros_like(s1_ref)
        s2_ref[...] = jnp.zeros_like(s2_ref)

    def chains(with_stats):
        for s in range(PROJ_TN // PROJ_SUB):
            cols = slice(s * PROJ_SUB, (s + 1) * PROJ_SUB)
            acc = _dot(hn_ref[...], w_ref[:, cols]) + b_ref[:, cols]
            z = 0.5 * acc * (1.0 + lax.erf(acc * np.float32(np.sqrt(0.5))))
            z_ref[0, :, cols] = z.astype(BF16)
            if with_stats:
                s1_ref[...] += jnp.sum(z, axis=-1, keepdims=True)
                s2_ref[...] += jnp.sum(z * z, axis=-1, keepdims=True)

    @pl.when(j < n_u)
    def _():
        chains(False)

    @pl.when(j >= n_u)
    def _():
        chains(True)

    @pl.when(j == n_tiles - 1)
    def _():
        mean = s1_ref[...] * (1.0 / SGU_E)
        var = s2_ref[...] * (1.0 / SGU_E) - mean * mean
        mean_ref[0] = mean
        rstd_ref[0] = lax.rsqrt(var + LN_EPS)


def _sgu_in(x, gain, w, b):
    B, S, D = x.shape
    N = w.shape[1]
    tm, tn = PROJ_TM, PROJ_TN
    n_tiles = N // tn
    kern = functools.partial(_sgu_in_kernel, n_u=SGU_E // tn, n_tiles=n_tiles)
    return pl.pallas_call(
        kern,
        grid=(B, S // tm, n_tiles),
        in_specs=[
            pl.BlockSpec((1, tm, D), lambda b_, i, j: (b_, i, 0)),
            pl.BlockSpec((1, D), lambda b_, i, j: (0, 0)),
            pl.BlockSpec((D, tn), lambda b_, i, j: (0, j)),
            pl.BlockSpec((1, tn), lambda b_, i, j: (0, j)),
        ],
        out_specs=[
            pl.BlockSpec((1, tm, tn), lambda b_, i, j: (b_, i, j)),
            pl.BlockSpec((1, tm, 1), lambda b_, i, j: (b_, i, 0)),
            pl.BlockSpec((1, tm, 1), lambda b_, i, j: (b_, i, 0)),
        ],
        out_shape=[
            jax.ShapeDtypeStruct((B, S, N), BF16),
            jax.ShapeDtypeStruct((B, S, 1), F32),
            jax.ShapeDtypeStruct((B, S, 1), F32),
        ],
        scratch_shapes=[
            pltpu.VMEM((tm, D), BF16),
            pltpu.VMEM((tm, 1), F32),
            pltpu.VMEM((tm, 1), F32),
        ],
        compiler_params=_params("parallel", "parallel", "arbitrary"),
        name="sgu_in",
    )(x, gain, w, b)


def _sgu_out_kernel(u_ref, v_ref, mean_ref, rstd_ref, lng_ref, lnb_ref, ws_ref,
                    bs_ref, wo_ref, x_ref, o_ref, acc_ref, gated_ref):
    g = pl.program_id(2)

    @pl.when(g == 0)
    def _():
        acc_ref[...] = jnp.zeros_like(acc_ref)

    vn = ((v_ref[0].astype(F32) - mean_ref[0]) * rstd_ref[0] * lng_ref[...]
          + lnb_ref[...]).astype(BF16)
    ws = ws_ref[0]
    bs = bs_ref[0]
    for c in range(SGU_TM // SGU_CHUNK):
        rows = slice(c * SGU_CHUNK, (c + 1) * SGU_CHUNK)
        mixed = _dot(ws, vn[rows]) + bs
        gated_ref[rows, :] = (u_ref[0, rows, :].astype(F32) * mixed).astype(BF16)
    acc_ref[...] += _dot(gated_ref[...], wo_ref[...])

    @pl.when(g == SGU_GROUPS - 1)
    def _():
        o_ref[0] = x_ref[0] + acc_ref[...]


def _sgu_out(z, mean, rstd, ln_g, ln_b, w_s, b_s, w_out, x):
    B, S, D = x.shape
    tm = SGU_TM
    G = SGU_GROUPS
    gw = SGU_GW
    return pl.pallas_call(
        _sgu_out_kernel,
        grid=(B, S // tm, G),
        in_specs=[
            pl.BlockSpec((1, tm, gw), lambda b, i, g: (b, i, g)),
            pl.BlockSpec((1, tm, gw), lambda b, i, g: (b, i, G + g)),
            pl.BlockSpec((1, tm, 1), lambda b, i, g: (b, i, 0)),
            pl.BlockSpec((1, tm, 1), lambda b, i, g: (b, i, 0)),
            pl.BlockSpec((1, gw), lambda b, i, g: (0, g)),
            pl.BlockSpec((1, gw), lambda b, i, g: (0, g)),
            pl.BlockSpec((1, SGU_CHUNK, SGU_CHUNK), lambda b, i, g: (g, 0, 0)),
            pl.BlockSpec((1, SGU_CHUNK, 1), lambda b, i, g: (g, 0, 0)),
            pl.BlockSpec((gw, D), lambda b, i, g: (g, 0)),
            pl.BlockSpec((1, tm, D), lambda b, i, g: (b, i, 0)),
        ],
        out_specs=pl.BlockSpec((1, tm, D), lambda b, i, g: (b, i, 0)),
        out_shape=jax.ShapeDtypeStruct((B, S, D), F32),
        scratch_shapes=[
            pltpu.VMEM((tm, D), F32),
            pltpu.VMEM((tm, gw), BF16),
        ],
        compiler_params=_params("parallel", "parallel", "arbitrary"),
        name="sgu_out",
    )(z, z, mean, rstd, ln_g, ln_b, w_s, b_s, w_out, x)


def _stage(a, scr_ref, first):
    tm = FFN_TM
    for c in range(FFN_HALF // LANES):
        blk = a[:, c * LANES:(c + 1) * LANES]
        scr_ref[first + c, 0:HALO, :] = blk[tm + HALO:tm + 2 * HALO]
        scr_ref[first + c, HALO:tm + 2 * HALO, :] = blk[:tm + HALO]


def _conv3(scr_ref, slab, cw, cb):
    tm = FFN_TM
    prev = scr_ref[slab, HALO - 1:HALO - 1 + tm, :]
    cur = scr_ref[slab, HALO:HALO + tm, :]
    nxt = scr_ref[slab, HALO + 1:HALO + 1 + tm, :]
    return prev * cw[0:1, :] + cur * cw[1:2, :] + nxt * cw[2:3, :] + cb


def _conv_ffn_kernel(x_ref, xprev_ref, xnext_ref, gain_ref, wg_ref, wu_ref,
                     cwg_ref, cwu_ref, cbg_ref, cbu_ref, wo_ref, fin_ref, o_ref,
                     hn_ref, acc_ref, acta_ref, actb_ref, sa_ref, sb_ref,
                     *, n_row_tiles, n_f, final_norm):
    i = pl.program_id(1)
    j = pl.program_id(2)
    tm = FFN_TM
    nblk = FFN_HALF // LANES

    @pl.when(j == 0)
    def _():
        gain = gain_ref[...]
        hn_ref[0:tm, :] = _rms_rows(x_ref[0], gain).astype(BF16)
        has_next = (i < n_row_tiles - 1).astype(F32)
        has_prev = (i > 0).astype(F32)
        halo = jnp.concatenate([_rms_rows(xnext_ref[0], gain) * has_next,
                                _rms_rows(xprev_ref[0], gain) * has_prev], axis=0)
        hn_ref[tm:tm + 2 * HALO, :] = halo.astype(BF16)
        acc_ref[...] = jnp.zeros_like(acc_ref)

    def in_proj(lo, scr_ref):
        hn = hn_ref[...]
        _stage(_dot(hn, wg_ref[:, lo:lo + FFN_HALF]), scr_ref, 0)
        _stage(_dot(hn, wu_ref[:, lo:lo + FFN_HALF]), scr_ref, nblk)

    def conv_act(scr_ref, lo, act_ref):
        for c in range(nblk):
            cols = slice(lo + c * LANES, lo + (c + 1) * LANES)
            gate = _conv3(scr_ref, c, cwg_ref[:, cols], cbg_ref[:, cols])
            up = _conv3(scr_ref, nblk + c, cwu_ref[:, cols], cbu_ref[:, cols])
            act_ref[:, c * LANES:(c + 1) * LANES] = (_silu(gate) * up).astype(BF16)

    in_proj(0, sa_ref)
    in_proj(FFN_HALF, sb_ref)
    conv_act(sa_ref, 0, acta_ref)
    acc_ref[...] += _dot(acta_ref[...], wo_ref[0:FFN_HALF, :])
    conv_act(sb_ref, FFN_HALF, actb_ref)
    acc_ref[...] += _dot(actb_ref[...], wo_ref[FFN_HALF:FFN_TF, :])

    @pl.when(j == n_f - 1)
    def _():
        y = x_ref[0] + acc_ref[...]
        if final_norm:
            y = _rms_rows(y, fin_ref[...])
        o_ref[0] = y


def _conv_ffn(x, gain, w_in, conv_w, conv_b, w_out, fin_gain, *, final_norm):
    B, S, D = x.shape
    tm, tf = FFN_TM, FFN_TF
    F = FFN_DIM
    n_f = F // tf
    n_row_tiles = S // tm
    hb = tm // HALO
    last_hb = S // HALO - 1
    kern = functools.partial(_conv_ffn_kernel, n_row_tiles=n_row_tiles, n_f=n_f,
                             final_norm=final_norm)
    return pl.pallas_call(
        kern,
        grid=(B, n_row_tiles, n_f),
        in_specs=[
            pl.BlockSpec((1, tm, D), lambda b, i, j: (b, i, 0)),
            pl.BlockSpec((1, HALO, D),
                         lambda b, i, j: (b, jnp.maximum(i * hb - 1, 0), 0)),
            pl.BlockSpec((1, HALO, D),
                         lambda b, i, j: (b, jnp.minimum((i + 1) * hb, last_hb), 0)),
            pl.BlockSpec((1, D), lambda b, i, j: (0, 0)),
            pl.BlockSpec((D, tf), lambda b, i, j: (0, j)),
            pl.BlockSpec((D, tf), lambda b, i, j: (0, n_f + j)),
            pl.BlockSpec((3, tf), lambda b, i, j: (0, j)),
            pl.BlockSpec((3, tf), lambda b, i, j: (0, n_f + j)),
            pl.BlockSpec((1, tf), lambda b, i, j: (0, j)),
            pl.BlockSpec((1, tf), lambda b, i, j: (0, n_f + j)),
            pl.BlockSpec((tf, D), lambda b, i, j: (j, 0)),
            pl.BlockSpec((1, D), lambda b, i, j: (0, 0)),
        ],
        out_specs=pl.BlockSpec((1, tm, D), lambda b, i, j: (b, i, 0)),
        out_shape=jax.ShapeDtypeStruct((B, S, D), F32),
        scratch_shapes=[
            pltpu.VMEM((tm + 2 * HALO, D), BF16),
            pltpu.VMEM((tm, D), F32),
            pltpu.VMEM((tm, FFN_HALF), BF16),
            pltpu.VMEM((tm, FFN_HALF), BF16),
            pltpu.VMEM((2 * FFN_HALF // LANES, tm + 2 * HALO, LANES), F32),
            pltpu.VMEM((2 * FFN_HALF // LANES, tm + 2 * HALO, LANES), F32),
        ],
        compiler_params=_params("parallel", "parallel", "arbitrary"),
        name="conv_ffn",
    )(x, x, x, gain, w_in, w_in, conv_w, conv_w, conv_b, conv_b, w_out, fin_gain)


def _rope_tables(S):
    half = RET_DK // 2
    inv = ROPE_BASE ** (-jnp.arange(half, dtype=F32) / half)
    ang = jnp.arange(S, dtype=F32)[:, None] * inv[None, :]
    return jnp.cos(ang), jnp.sin(ang)


def _row(v):
    return v.reshape(1, -1)


def _prepare(ln_mix, ln_ffn, ret_w_in, ret_log_decay_fwd, ret_log_decay_bwd,
             ret_gn_gain, ret_w_out, sgu_w_in, sgu_b_in, sgu_ln_g, sgu_ln_b,
             sgu_w_s, sgu_b_s, sgu_w_out, ffn_w_in, ffn_conv_w, ffn_conv_b,
             ffn_w_out, ln_final):
    depth = ln_mix.shape[0]
    hq = RET_HQ
    layers = []
    for i in range(depth):
        j = i // 2
        p = {"ln_mix": _row(ln_mix[i]), "ln_ffn": _row(ln_ffn[i])}
        if i % 2 == 0:
            w = ret_w_in[j]
            p["w_vgq"] = jnp.concatenate([w[:, 2 * hq:], w[:, :hq]], axis=1).astype(BF16)
            p["w_kT"] = w[:, hq:2 * hq].T.astype(BF16)
            p["lg_fwd"] = ret_log_decay_fwd[j]
            p["lg_bwd"] = ret_log_decay_bwd[j]
            p["gn_gain"] = _row(ret_gn_gain[j])
            p["w_out"] = ret_w_out[j].astype(BF16)
        else:
            p["w_in"] = sgu_w_in[j].astype(BF16)
            p["b_in"] = _row(sgu_b_in[j])
            p["ln_g"] = _row(sgu_ln_g[j])
            p["ln_b"] = _row(sgu_ln_b[j])
            p["w_s"] = sgu_w_s[j].astype(BF16)
            p["b_s"] = sgu_b_s[j][:, :, None]
            p["w_out"] = sgu_w_out[j].astype(BF16)
        p["ffn_w_in"] = ffn_w_in[i].astype(BF16)
        p["ffn_conv_w"] = ffn_conv_w[i]
        p["ffn_conv_b"] = _row(ffn_conv_b[i])
        p["ffn_w_out"] = ffn_w_out[i].astype(BF16)
        layers.append(p)
    return layers, _row(ln_final)


def _trunk(x, layers, fin_gain):
    S = x.shape[1]
    cos, sin = _rope_tables(S)
    cosT, sinT = cos.T, sin.T
    for i, p in enumerate(layers):
        if i % 2 == 0:
            vgq = _ret_proj(x, p["ln_mix"], p["w_vgq"], cos, sin)
            kT = _ret_projT(x, p["ln_mix"], p["w_kT"], cosT, sinT)
            y_rev = _retention(vgq, kT, p["lg_bwd"], reverse=True)
            a = _retention(vgq, kT, p["lg_fwd"], reverse=False, y_other=y_rev,
                           gn_gain=p["gn_gain"])
            x = _out_proj(a, p["w_out"], x)
        else:
            z, mean, rstd = _sgu_in(x, p["ln_mix"], p["w_in"], p["b_in"])
            x = _sgu_out(z, mean, rstd, p["ln_g"], p["ln_b"], p["w_s"], p["b_s"],
                         p["w_out"], x)
        x = _conv_ffn(x, p["ln_ffn"], p["ffn_w_in"], p["ffn_conv_w"],
                      p["ffn_conv_b"], p["ffn_w_out"], fin_gain,
                      final_norm=(i == len(layers) - 1))
    return x


def kernel(x_prompt, x_sample, ln_mix, ln_ffn, ret_w_in, ret_log_decay_fwd, ret_log_decay_bwd, ret_gn_gain, ret_w_out, sgu_w_in, sgu_b_in, sgu_ln_g, sgu_ln_b, sgu_w_s, sgu_b_s, sgu_w_out, ffn_w_in, ffn_conv_w, ffn_conv_b, ffn_w_out, ln_final):
    layers, fin_gain = _prepare(
        ln_mix, ln_ffn, ret_w_in, ret_log_decay_fwd, ret_log_decay_bwd, ret_gn_gain,
        ret_w_out, sgu_w_in, sgu_b_in, sgu_ln_g, sgu_ln_b, sgu_w_s, sgu_b_s,
        sgu_w_out, ffn_w_in, ffn_conv_w, ffn_conv_b, ffn_w_out, ln_final)
    return (_trunk(x_prompt, layers, fin_gain), _trunk(x_sample, layers, fin_gain))
```

```python
import functools

import jax
import jax.numpy as jnp
import numpy as np
from jax import lax
from jax.experimental import pallas as pl
from jax.experimental.pallas import tpu as pltpu

D_MODEL = 2048
RET_HEADS = 8
RET_DK = D_MODEL // RET_HEADS
RET_DV = 2 * D_MODEL // RET_HEADS
RET_HQ = RET_HEADS * RET_DK
RET_HV = RET_HEADS * RET_DV
ROPE_BASE = 10000.0
SGU_E = 3 * D_MODEL
SGU_GROUPS = 8
SGU_CHUNK = 128
SGU_GW = SGU_E // SGU_GROUPS
FFN_DIM = 5632
EPS = 1e-6
LN_EPS = 1e-5

SUBLANES_F32 = 8
LANES = 128
VMEM_LIMIT_BYTES = 56 * 1024 * 1024

PROJ_TM = 1024
PROJ_TN = 1024
PROJ_SUB = 512
KT_TN = 512
RET_BLOCK = 256
OUT_TM = 512
SGU_TM = 512
FFN_TM = 512
FFN_TF = 512
FFN_HALF = FFN_TF // 2
HALO = SUBLANES_F32

BF16 = jnp.bfloat16
F32 = jnp.float32


def _params(*semantics):
    return pltpu.CompilerParams(dimension_semantics=semantics,
                                vmem_limit_bytes=VMEM_LIMIT_BYTES)


def _rms_rows(x, gain):
    ms = jnp.mean(x * x, axis=-1, keepdims=True)
    return x * lax.rsqrt(ms + EPS) * gain


def _dot(a, b):
    return jnp.dot(a, b, preferred_element_type=F32)


def _silu(a):
    return a * jax.nn.sigmoid(a)


def _ret_proj_kernel(x_ref, gain_ref, w_ref, wt_ref, cos_ref, sin_ref, cost_ref,
                     sint_ref, o_ref, kt_ref, hn_ref, *, n_plain, n_silu, n_rope):
    j = pl.program_id(2)
    C = RET_BLOCK

    @pl.when(j == 0)
    def _():
        hn_ref[...] = _rms_rows(x_ref[0], gain_ref[...]).astype(BF16)

    def keys_transposed():
        cos = cost_ref[...]
        sin = sint_ref[...]
        half = RET_DK // 2
        scale = RET_DK ** -0.5
        for s in range(PROJ_TN // PROJ_SUB):
            r0 = s * PROJ_SUB
            acc = lax.dot_general(wt_ref[r0:r0 + PROJ_SUB, :], hn_ref[...],
                                  (((1,), (1,)), ((), ())),
                                  preferred_element_type=F32)
            for hh in range(PROJ_SUB // RET_DK):
                a = hh * RET_DK
                x1 = acc[a:a + half, :]
                x2 = acc[a + half:a + RET_DK, :]
                r1 = ((x1 * cos - x2 * sin) * scale).astype(BF16)
                r2 = ((x2 * cos + x1 * sin) * scale).astype(BF16)
                for c in range(PROJ_TM // C):
                    kt_ref[0, c, r0 + a:r0 + a + half, :] = r1[:, c * C:(c + 1) * C]
                    kt_ref[0, c, r0 + a + half:r0 + a + RET_DK, :] = r2[:, c * C:(c + 1) * C]

    def store_plain(acc, lo):
        o_ref[0, :, lo:lo + PROJ_SUB] = acc.astype(BF16)

    def store_silu(acc, lo):
        o_ref[0, :, lo:lo + PROJ_SUB] = _silu(acc).astype(BF16)

    def store_rope(acc, lo):
        cos = cos_ref[...]
        sin = sin_ref[...]
        half = RET_DK // 2
        for hh in range(PROJ_SUB // RET_DK):
            a = hh * RET_DK
            x1 = acc[:, a:a + half]
            x2 = acc[:, a + half:a + RET_DK]
            o_ref[0, :, lo + a:lo + a + half] = (x1 * cos - x2 * sin).astype(BF16)
            o_ref[0, :, lo + a + half:lo + a + RET_DK] = (x2 * cos + x1 * sin).astype(BF16)

    def chains(store):
        for s in range(PROJ_TN // PROJ_SUB):
            lo = s * PROJ_SUB
            store(_dot(hn_ref[...], w_ref[:, lo:lo + PROJ_SUB]), lo)

    @pl.when(j < n_plain)
    def _():
        chains(store_plain)

    @pl.when(jnp.logical_and(j >= n_plain, j < n_plain + n_silu))
    def _():
        chains(store_silu)

    @pl.when(jnp.logical_and(j >= n_plain + n_silu, j < n_plain + n_silu + n_rope))
    def _():
        chains(store_rope)

    @pl.when(j >= n_plain + n_silu + n_rope)
    def _():
        keys_transposed()


def _ret_proj(x, gain, w_vgq, w_kT, cos, sin, cosT, sinT):
    B, S, D = x.shape
    N = w_vgq.shape[1]
    tm, tn, C = PROJ_TM, PROJ_TN, RET_BLOCK
    n_main = N // tn
    n_keys = RET_HQ // tn
    half = RET_DK // 2
    kern = functools.partial(_ret_proj_kernel, n_plain=RET_HV // tn, n_silu=RET_HV // tn,
                             n_rope=RET_HQ // tn)
    main = lambda j: jnp.minimum(j, n_main - 1)
    keys = lambda j: jnp.maximum(j - n_main, 0)
    return pl.pallas_call(
        kern,
        grid=(B, S // tm, n_main + n_keys),
        in_specs=[
            pl.BlockSpec((1, tm, D), lambda b, i, j: (b, i, 0)),
            pl.BlockSpec((1, D), lambda b, i, j: (0, 0)),
            pl.BlockSpec((D, tn), lambda b, i, j: (0, main(j))),
            pl.BlockSpec((tn, D), lambda b, i, j: (keys(j), 0)),
            pl.BlockSpec((tm, half), lambda b, i, j: (i, 0)),
            pl.BlockSpec((tm, half), lambda b, i, j: (i, 0)),
            pl.BlockSpec((half, tm), lambda b, i, j: (0, i)),
            pl.BlockSpec((half, tm), lambda b, i, j: (0, i)),
        ],
        out_specs=[
            pl.BlockSpec((1, tm, tn), lambda b, i, j: (b, i, main(j))),
            pl.BlockSpec((1, tm // C, tn, C), lambda b, i, j: (b, i, keys(j), 0)),
        ],
        out_shape=[
            jax.ShapeDtypeStruct((B, S, N), BF16),
            jax.ShapeDtypeStruct((B, S // C, RET_HQ, C), BF16),
        ],
        scratch_shapes=[pltpu.VMEM((tm, D), BF16)],
        compiler_params=_params("parallel", "parallel", "arbitrary"),
        name="ret_proj",
    )(x, gain, w_vgq, w_kT, cos, sin, cosT, sinT)


def _retention_kernel(lg_ref, q_ref, kt_ref, v_ref, *rest, reverse, finish):
    if finish:
        yb_ref, gate_ref, gain_ref, o_ref = rest[:4]
        state_ref, dec_ref, qd_ref, kd_ref, cd_ref = rest[4:]
    else:
        o_ref = rest[0]
        state_ref, dec_ref, qd_ref, kd_ref, cd_ref = rest[1:]
    C = RET_BLOCK
    t = pl.program_id(1)

    @pl.when(t == 0)
    def _():
        state_ref[...] = jnp.zeros_like(state_ref)
        ri = lax.broadcasted_iota(jnp.int32, (C, C), 0)
        ci = lax.broadcasted_iota(jnp.int32, (C, C), 1)
        d = (ci - ri) if reverse else (ri - ci)
        mask = (d > 0) if reverse else (d >= 0)
        dist = jnp.maximum(d, 0).astype(F32)
        row = lax.broadcasted_iota(jnp.int32, (C, 1), 0).astype(F32)
        col = lax.broadcasted_iota(jnp.int32, (1, C), 1).astype(F32)
        for h in range(RET_HEADS):
            lg = lg_ref[h]
            dec_ref[h] = jnp.where(mask, jnp.exp(lg * dist), 0.0)
            if reverse:
                qd_ref[h] = jnp.exp(lg * (C - row))
                kd_ref[h] = jnp.exp(lg * col)
            else:
                qd_ref[h] = jnp.exp(lg * (row + 1.0))
                kd_ref[h] = jnp.exp(lg * (C - 1.0 - col))
            cd_ref[h] = jnp.exp(jnp.full((1, RET_DV), lg * C, F32))

    for h in range(RET_HEADS):
        kcols = slice(h * RET_DK, (h + 1) * RET_DK)
        vcols = slice(h * RET_DV, (h + 1) * RET_DV)
        q = q_ref[0, :, kcols]
        kt = kt_ref[0, 0, kcols, :]
        v = v_ref[0, :, vcols]
        scores = _dot(q, kt) * dec_ref[h]
        y = _dot(scores.astype(BF16), v)
        y = y + qd_ref[h] * _dot(q, state_ref[h].astype(BF16))
        kts = (kt.astype(F32) * kd_ref[h]).astype(BF16)
        state_ref[h] = state_ref[h] * cd_ref[h] + _dot(kts, v)
        if finish:
            y = y + yb_ref[0, :, vcols].astype(F32)
            mu = jnp.mean(y, axis=-1, keepdims=True)
            yc = y - mu
            var = jnp.mean(yc * yc, axis=-1, keepdims=True)
            yn = yc * lax.rsqrt(var + LN_EPS) * gain_ref[:, vcols]
            o_ref[0, :, vcols] = (gate_ref[0, :, vcols].astype(F32) * yn).astype(BF16)
        else:
            o_ref[0, :, vcols] = y.astype(BF16)


def _retention(vgq, kT, log_gamma, *, reverse, y_other=None, gn_gain=None):
    B, S, _ = vgq.shape
    C = RET_BLOCK
    H = RET_HEADS
    nC = S // C
    finish = y_other is not None
    q_blk = 2 * RET_HV // RET_HQ
    if reverse:
        cidx = lambda t: nC - 1 - t
    else:
        cidx = lambda t: t
    in_specs = [
        pl.BlockSpec(memory_space=pltpu.SMEM),
        pl.BlockSpec((1, C, RET_HQ), lambda b, t: (b, cidx(t), q_blk)),
        pl.BlockSpec((1, 1, RET_HQ, C), lambda b, t: (b, cidx(t), 0, 0)),
        pl.BlockSpec((1, C, RET_HV), lambda b, t: (b, cidx(t), 0)),
    ]
    args = [log_gamma, vgq, kT, vgq]
    if finish:
        in_specs += [
            pl.BlockSpec((1, C, RET_HV), lambda b, t: (b, cidx(t), 0)),
            pl.BlockSpec((1, C, RET_HV), lambda b, t: (b, cidx(t), 1)),
            pl.BlockSpec((1, RET_HV), lambda b, t: (0, 0)),
        ]
        args += [y_other, vgq, gn_gain]
    kern = functools.partial(_retention_kernel, reverse=reverse, finish=finish)
    return pl.pallas_call(
        kern,
        grid=(B, nC),
        in_specs=in_specs,
        out_specs=pl.BlockSpec((1, C, RET_HV), lambda b, t: (b, cidx(t), 0)),
        out_shape=jax.ShapeDtypeStruct((B, S, RET_HV), BF16),
        scratch_shapes=[
            pltpu.VMEM((H, RET_DK, RET_DV), F32),
            pltpu.VMEM((H, C, C), F32),
            pltpu.VMEM((H, C, 1), F32),
            pltpu.VMEM((H, 1, C), F32),
            pltpu.VMEM((H, 1, RET_DV), F32),
        ],
        compiler_params=_params("parallel", "arbitrary"),
        name="retention_rev" if reverse else "retention_fwd",
    )(*args)


def _out_proj_kernel(a_ref, w_ref, x_ref, o_ref):
    N = w_ref.shape[1]
    for s in range(N // PROJ_SUB):
        cols = slice(s * PROJ_SUB, (s + 1) * PROJ_SUB)
        o_ref[0, :, cols] = x_ref[0, :, cols] + _dot(a_ref[0], w_ref[:, cols])


def _out_proj(a, w, x):
    B, S, K = a.shape
    N = w.shape[1]
    tm = OUT_TM
    return pl.pallas_call(
        _out_proj_kernel,
        grid=(B, S // tm),
        in_specs=[
            pl.BlockSpec((1, tm, K), lambda b, i: (b, i, 0)),
            pl.BlockSpec((K, N), lambda b, i: (0, 0), pipeline_mode=pl.Buffered(1)),
            pl.BlockSpec((1, tm, N), lambda b, i: (b, i, 0)),
        ],
        out_specs=pl.BlockSpec((1, tm, N), lambda b, i: (b, i, 0)),
        out_shape=jax.ShapeDtypeStruct((B, S, N), F32),
        compiler_params=_params("parallel", "parallel"),
        name="out_proj",
    )(a, w, x)


def _sgu_in_kernel(x_ref, gain_ref, w_ref, b_ref, z_ref, mean_ref, rstd_ref,
                   hn_ref, s1_ref, s2_ref, *, n_u, n_tiles):
    j = pl.program_id(2)

    @pl.when(j == 0)
    def _():
        hn_ref[...] = _rms_rows(x_ref[0], gain_ref[...]).astype(BF16)
        s1_ref[...] = jnp.zeros_like(s1_ref)
        s2_ref[...] = jnp.zeros_like(s2_ref)

    def chains(with_stats):
        for s in range(PROJ_TN // PROJ_SUB):
            cols = slice(s * PROJ_SUB, (s + 1) * PROJ_SUB)
            acc = _dot(hn_ref[...], w_ref[:, cols]) + b_ref[:, cols]
            z = 0.5 * acc * (1.0 + lax.erf(acc * np.float32(np.sqrt(0.5))))
            z_ref[0, :, cols] = z.astype(BF16)
            if with_stats:
                s1_ref[...] += jnp.sum(z, axis=-1, keepdims=True)
                s2_ref[...] += jnp.sum(z * z, axis=-1, keepdims=True)

    @pl.when(j < n_u)
    def _():
        chains(False)

    @pl.when(j >= n_u)
    def _():
        chains(True)

    @pl.when(j == n_tiles - 1)
    def _():
        mean = s1_ref[...] * (1.0 / SGU_E)
        var = s2_ref[...] * (1.0 / SGU_E) - mean * mean
        mean_ref[0] = mean
        rstd_ref[0] = lax.rsqrt(var + LN_EPS)


def _sgu_in(x, gain, w, b):
    B, S, D = x.shape
    N = w.shape[1]
    tm, tn = PROJ_TM, PROJ_TN
    n_tiles = N // tn
    kern = functools.partial(_sgu_in_kernel, n_u=SGU_E // tn, n_tiles=n_tiles)
    return pl.pallas_call(
        kern,
        grid=(B, S // tm, n_tiles),
        in_specs=[
            pl.BlockSpec((1, tm, D), lambda b_, i, j: (b_, i, 0)),
            pl.BlockSpec((1, D), lambda b_, i, j: (0, 0)),
            pl.BlockSpec((D, tn), lambda b_, i, j: (0, j)),
            pl.BlockSpec((1, tn), lambda b_, i, j: (0, j)),
        ],
        out_specs=[
            pl.BlockSpec((1, tm, tn), lambda b_, i, j: (b_, i, j)),
            pl.BlockSpec((1, tm, 1), lambda b_, i, j: (b_, i, 0)),
            pl.BlockSpec((1, tm, 1), lambda b_, i, j: (b_, i, 0)),
        ],
        out_shape=[
            jax.ShapeDtypeStruct((B, S, N), BF16),
            jax.ShapeDtypeStruct((B, S, 1), F32),
            jax.ShapeDtypeStruct((B, S, 1), F32),
        ],
        scratch_shapes=[
            pltpu.VMEM((tm, D), BF16),
            pltpu.VMEM((tm, 1), F32),
            pltpu.VMEM((tm, 1), F32),
        ],
        compiler_params=_params("parallel", "parallel", "arbitrary"),
        name="sgu_in",
    )(x, gain, w, b)


def _sgu_out_kernel(u_ref, v_ref, mean_ref, rstd_ref, lng_ref, lnb_ref, ws_ref,
                    bs_ref, wo_ref, wo_tail_ref, x_ref, o_ref, acc_ref, ga_ref, gb_ref,
                    *, n_pairs):
    s = pl.program_id(2)
    gw = SGU_GW

    @pl.when(s == 0)
    def _():
        acc_ref[...] = jnp.zeros_like(acc_ref)
        gb_ref[...] = jnp.zeros_like(gb_ref)

    def gate_group(k, out_ref):
        g = 2 * s + k
        cols = slice(k * gw, (k + 1) * gw)
        vn = ((v_ref[0, :, cols].astype(F32) - mean_ref[0]) * rstd_ref[0] * lng_ref[g]
              + lnb_ref[g]).astype(BF16)
        ws = ws_ref[g]
        bs = bs_ref[g]
        for c in range(SGU_TM // SGU_CHUNK):
            rows = slice(c * SGU_CHUNK, (c + 1) * SGU_CHUNK)
            mixed = _dot(ws, vn[rows]) + bs
            out_ref[rows, :] = (u_ref[0, rows, cols].astype(F32) * mixed).astype(BF16)

    acc_ref[...] += _dot(gb_ref[...], wo_ref[0:gw, :])
    gate_group(0, ga_ref)
    acc_ref[...] += _dot(ga_ref[...], wo_ref[gw:2 * gw, :])
    gate_group(1, gb_ref)

    @pl.when(s == n_pairs - 1)
    def _():
        o_ref[0] = x_ref[0] + acc_ref[...] + _dot(gb_ref[...], wo_tail_ref[...])


def _sgu_out(z, mean, rstd, ln_g, ln_b, w_s, b_s, w_out_shifted, x):
    B, S, D = x.shape
    tm = SGU_TM
    G = SGU_GROUPS
    gw = SGU_GW
    n_pairs = G // 2
    kern = functools.partial(_sgu_out_kernel, n_pairs=n_pairs)
    whole = lambda a: pl.BlockSpec(a.shape, lambda b, i, s: (0,) * a.ndim)
    return pl.pallas_call(
        kern,
        grid=(B, S // tm, n_pairs),
        in_specs=[
            pl.BlockSpec((1, tm, 2 * gw), lambda b, i, s: (b, i, s)),
            pl.BlockSpec((1, tm, 2 * gw), lambda b, i, s: (b, i, n_pairs + s)),
            pl.BlockSpec((1, tm, 1), lambda b, i, s: (b, i, 0)),
            pl.BlockSpec((1, tm, 1), lambda b, i, s: (b, i, 0)),
            whole(ln_g),
            whole(ln_b),
            whole(w_s),
            whole(b_s),
            pl.BlockSpec((2 * gw, D), lambda b, i, s: (s, 0)),
            pl.BlockSpec((gw, D), lambda b, i, s: (G, 0)),
            pl.BlockSpec((1, tm, D), lambda b, i, s: (b, i, 0)),
        ],
        out_specs=pl.BlockSpec((1, tm, D), lambda b, i, s: (b, i, 0)),
        out_shape=jax.ShapeDtypeStruct((B, S, D), F32),
        scratch_shapes=[
            pltpu.VMEM((tm, D), F32),
            pltpu.VMEM((tm, gw), BF16),
            pltpu.VMEM((tm, gw), BF16),
        ],
        compiler_params=_params("parallel", "parallel", "arbitrary"),
        name="sgu_out",
    )(z, z, mean, rstd, ln_g, ln_b, w_s, b_s, w_out_shifted, w_out_shifted, x)


def _stage(a, scr_ref, first):
    tm = FFN_TM
    for c in range(FFN_HALF // LANES):
        blk = a[:, c * LANES:(c + 1) * LANES]
        scr_ref[first + c, 0:HALO, :] = blk[tm + HALO:tm + 2 * HALO]
        scr_ref[first + c, HALO:tm + 2 * HALO, :] = blk[:tm + HALO]


def _conv3(scr_ref, slab, cw, cb):
    tm = FFN_TM
    prev = scr_ref[slab, HALO - 1:HALO - 1 + tm, :]
    cur = scr_ref[slab, HALO:HALO + tm, :]
    nxt = scr_ref[slab, HALO + 1:HALO + 1 + tm, :]
    return prev * cw[0:1, :] + cur * cw[1:2, :] + nxt * cw[2:3, :] + cb


def _conv_ffn_kernel(x_ref, xprev_ref, xnext_ref, gain_ref, w_ref, cw_ref, wo_ref,
                     fin_ref, o_ref,
                     hn_ref, acc_ref, acta_ref, actb_ref, sa_ref, sb_ref,
                     *, n_row_tiles, n_f, final_norm):
    i = pl.program_id(1)
    j = pl.program_id(2)
    tm = FFN_TM
    nblk = FFN_HALF // LANES
    up0 = FFN_TF

    @pl.when(j == 0)
    def _():
        gain = gain_ref[...]
        hn_ref[0:tm, :] = _rms_rows(x_ref[0], gain).astype(BF16)
        has_next = (i < n_row_tiles - 1).astype(F32)
        has_prev = (i > 0).astype(F32)
        halo = jnp.concatenate([_rms_rows(xnext_ref[0], gain) * has_next,
                                _rms_rows(xprev_ref[0], gain) * has_prev], axis=0)
        hn_ref[tm:tm + 2 * HALO, :] = halo.astype(BF16)
        acc_ref[...] = jnp.zeros_like(acc_ref)

    def in_proj(lo, scr_ref):
        hn = hn_ref[...]
        _stage(_dot(hn, w_ref[0, :, lo:lo + FFN_HALF]), scr_ref, 0)
        _stage(_dot(hn, w_ref[0, :, up0 + lo:up0 + lo + FFN_HALF]), scr_ref, nblk)

    def conv_act(scr_ref, lo, act_ref):
        cw = cw_ref[j]
        for c in range(nblk):
            g0 = lo + c * LANES
            u0 = up0 + g0
            gate = _conv3(scr_ref, c, cw[0:3, g0:g0 + LANES], cw[3:4, g0:g0 + LANES])
            up = _conv3(scr_ref, nblk + c, cw[0:3, u0:u0 + LANES], cw[3:4, u0:u0 + LANES])
            act_ref[:, c * LANES:(c + 1) * LANES] = (_silu(gate) * up).astype(BF16)

    in_proj(0, sa_ref)
    in_proj(FFN_HALF, sb_ref)
    conv_act(sa_ref, 0, acta_ref)
    acc_ref[...] += _dot(acta_ref[...], wo_ref[0:FFN_HALF, :])
    conv_act(sb_ref, FFN_HALF, actb_ref)
    acc_ref[...] += _dot(actb_ref[...], wo_ref[FFN_HALF:FFN_TF, :])

    @pl.when(j == n_f - 1)
    def _():
        y = x_ref[0] + acc_ref[...]
        if final_norm:
            y = _rms_rows(y, fin_ref[...])
        o_ref[0] = y


def _conv_ffn(x, gain, w_in_tiles, conv_tiles, w_out, fin_gain, *, final_norm):
    B, S, D = x.shape
    tm, tf = FFN_TM, FFN_TF
    n_f = w_in_tiles.shape[0]
    n_row_tiles = S // tm
    hb = tm // HALO
    last_hb = S // HALO - 1
    kern = functools.partial(_conv_ffn_kernel, n_row_tiles=n_row_tiles, n_f=n_f,
                             final_norm=final_norm)
    return pl.pallas_call(
        kern,
        grid=(B, n_row_tiles, n_f),
        in_specs=[
            pl.BlockSpec((1, tm, D), lambda b, i, j: (b, i, 0)),
            pl.BlockSpec((1, HALO, D),
                         lambda b, i, j: (b, jnp.maximum(i * hb - 1, 0), 0)),
            pl.BlockSpec((1, HALO, D),
                         lambda b, i, j: (b, jnp.minimum((i + 1) * hb, last_hb), 0)),
            pl.BlockSpec((1, D), lambda b, i, j: (0, 0)),
            pl.BlockSpec((1, D, 2 * tf), lambda b, i, j: (j, 0, 0)),
            pl.BlockSpec(conv_tiles.shape, lambda b, i, j: (0, 0, 0)),
            pl.BlockSpec((tf, D), lambda b, i, j: (j, 0)),
            pl.BlockSpec((1, D), lambda b, i, j: (0, 0)),
        ],
        out_specs=pl.BlockSpec((1, tm, D), lambda b, i, j: (b, i, 0)),
        out_shape=jax.ShapeDtypeStruct((B, S, D), F32),
        scratch_shapes=[
            pltpu.VMEM((tm + 2 * HALO, D), BF16),
            pltpu.VMEM((tm, D), F32),
            pltpu.VMEM((tm, FFN_HALF), BF16),
            pltpu.VMEM((tm, FFN_HALF), BF16),
            pltpu.VMEM((2 * FFN_HALF // LANES, tm + 2 * HALO, LANES), F32),
            pltpu.VMEM((2 * FFN_HALF // LANES, tm + 2 * HALO, LANES), F32),
        ],
        compiler_params=_params("parallel", "parallel", "arbitrary"),
        name="conv_ffn",
    )(x, x, x, gain, w_in_tiles, conv_tiles, w_out, fin_gain)


def _ffn_tiles(w_in, conv_w, conv_b):
    def tiles(a):
        r = a.shape[0]
        a = a.reshape(r, 2, FFN_DIM // FFN_TF, FFN_TF)
        return a.transpose(2, 0, 1, 3).reshape(FFN_DIM // FFN_TF, r, 2 * FFN_TF)
    conv = jnp.concatenate([conv_w, conv_b.reshape(1, -1)], axis=0)
    return tiles(w_in.astype(BF16)), tiles(conv)


def _rope_tables(S):
    half = RET_DK // 2
    inv = ROPE_BASE ** (-jnp.arange(half, dtype=F32) / half)
    ang = jnp.arange(S, dtype=F32)[:, None] * inv[None, :]
    return jnp.cos(ang), jnp.sin(ang)


def _row(v):
    return v.reshape(1, -1)


def _prepare(ln_mix, ln_ffn, ret_w_in, ret_log_decay_fwd, ret_log_decay_bwd,
             ret_gn_gain, ret_w_out, sgu_w_in, sgu_b_in, sgu_ln_g, sgu_ln_b,
             sgu_w_s, sgu_b_s, sgu_w_out, ffn_w_in, ffn_conv_w, ffn_conv_b,
             ffn_w_out, ln_final):
    depth = ln_mix.shape[0]
    hq = RET_HQ
    layers = []
    for i in range(depth):
        j = i // 2
        p = {"ln_mix": _row(ln_mix[i]), "ln_ffn": _row(ln_ffn[i])}
        if i % 2 == 0:
            w = ret_w_in[j]
            p["w_vgq"] = jnp.concatenate([w[:, 2 * hq:], w[:, :hq]], axis=1).astype(BF16)
            p["w_kT"] = w[:, hq:2 * hq].T.astype(BF16)
            p["lg_fwd"] = ret_log_decay_fwd[j]
            p["lg_bwd"] = ret_log_decay_bwd[j]
            p["gn_gain"] = _row(ret_gn_gain[j])
            p["w_out"] = ret_w_out[j].astype(BF16)
        else:
            p["w_in"] = sgu_w_in[j].astype(BF16)
            p["b_in"] = _row(sgu_b_in[j])
            p["ln_g"] = sgu_ln_g[j].reshape(SGU_GROUPS, 1, SGU_GW)
            p["ln_b"] = sgu_ln_b[j].reshape(SGU_GROUPS, 1, SGU_GW)
            p["w_s"] = sgu_w_s[j].astype(BF16)
            p["b_s"] = sgu_b_s[j][:, :, None]
            p["w_out"] = jnp.pad(sgu_w_out[j].astype(BF16), ((SGU_GW, 0), (0, 0)))
        p["ffn_w_in"], p["ffn_conv"] = _ffn_tiles(ffn_w_in[i], ffn_conv_w[i], ffn_conv_b[i])
        p["ffn_w_out"] = ffn_w_out[i].astype(BF16)
        layers.append(p)
    return layers, _row(ln_final)


def _trunk(x, layers, fin_gain):
    S = x.shape[1]
    cos, sin = _rope_tables(S)
    cosT, sinT = cos.T, sin.T
    for i, p in enumerate(layers):
        if i % 2 == 0:
            vgq, kT = _ret_proj(x, p["ln_mix"], p["w_vgq"], p["w_kT"], cos, sin, cosT, sinT)
            y_rev = _retention(vgq, kT, p["lg_bwd"], reverse=True)
            a = _retention(vgq, kT, p["lg_fwd"], reverse=False, y_other=y_rev,
                           gn_gain=p["gn_gain"])
            x = _out_proj(a, p["w_out"], x)
        else:
            z, mean, rstd = _sgu_in(x, p["ln_mix"], p["w_in"], p["b_in"])
            x = _sgu_out(z, mean, rstd, p["ln_g"], p["ln_b"], p["w_s"], p["b_s"],
                         p["w_out"], x)
        x = _conv_ffn(x, p["ln_ffn"], p["ffn_w_in"], p["ffn_conv"], p["ffn_w_out"],
                      fin_gain, final_norm=(i == len(layers) - 1))
    return x


def kernel(x_prompt, x_sample, ln_mix, ln_ffn, ret_w_in, ret_log_decay_fwd, ret_log_decay_bwd, ret_gn_gain, ret_w_out, sgu_w_in, sgu_b_in, sgu_ln_g, sgu_ln_b, sgu_w_s, sgu_b_s, sgu_w_out, ffn_w_in, ffn_conv_w, ffn_conv_b, ffn_w_out, ln_final):
    layers, fin_gain = _prepare(
        ln_mix, ln_ffn, ret_w_in, ret_log_decay_fwd, ret_log_decay_bwd, ret_gn_gain,
        ret_w_out, sgu_w_in, sgu_b_in, sgu_ln_g, sgu_ln_b, sgu_w_s, sgu_b_s,
        sgu_w_out, ffn_w_in, ffn_conv_w, ffn_conv_b, ffn_w_out, ln_final)
    return (_trunk(x_prompt, layers, fin_gain), _trunk(x_sample, layers, fin_gain))
```

```python
import functools

import jax
import jax.numpy as jnp
import numpy as np
from jax import lax
from jax.experimental import pallas as pl
from jax.experimental.pallas import tpu as pltpu

D_MODEL = 2048
RET_HEADS = 8
RET_DK = D_MODEL // RET_HEADS
RET_DV = 2 * D_MODEL // RET_HEADS
RET_HQ = RET_HEADS * RET_DK
RET_HV = RET_HEADS * RET_DV
ROPE_BASE = 10000.0
SGU_E = 3 * D_MODEL
SGU_GROUPS = 8
SGU_CHUNK = 128
SGU_GW = SGU_E // SGU_GROUPS
FFN_DIM = 5632
EPS = 1e-6
LN_EPS = 1e-5

SUBLANES_F32 = 8
LANES = 128
VMEM_LIMIT_BYTES = 60 * 1024 * 1024

PROJ_TM = 1024
PROJ_TN = 1024
PROJ_SUB = 512
RET_BLOCK = 256
OUT_TM = 512
SGU_TM = 512
FFN_TM = 1024
FFN_TF = 512
FFN_HALF = FFN_TF // 2
HALO = SUBLANES_F32

BF16 = jnp.bfloat16
F32 = jnp.float32


def _params(*semantics):
    return pltpu.CompilerParams(dimension_semantics=semantics,
                                vmem_limit_bytes=VMEM_LIMIT_BYTES)


def _rms_rows(x, gain):
    ms = jnp.mean(x * x, axis=-1, keepdims=True)
    return x * lax.rsqrt(ms + EPS) * gain


def _dot(a, b):
    return jnp.dot(a, b, preferred_element_type=F32)


def _silu(a):
    return a * jax.nn.sigmoid(a)


def _ret_proj_kernel(x_ref, gain_ref, w_ref, cos_ref, sin_ref, o_ref, hn_ref,
                     *, n_plain, n_silu, n_query):
    j = pl.program_id(2)

    @pl.when(j == 0)
    def _():
        hn_ref[...] = _rms_rows(x_ref[0], gain_ref[...]).astype(BF16)

    def store_plain(acc, lo):
        o_ref[0, :, lo:lo + PROJ_SUB] = acc.astype(BF16)

    def store_silu(acc, lo):
        o_ref[0, :, lo:lo + PROJ_SUB] = _silu(acc).astype(BF16)

    def store_rope(acc, lo):
        cos = cos_ref[...]
        sin = sin_ref[...]
        half = RET_DK // 2
        scale = jnp.where(j >= n_plain + n_silu + n_query, RET_DK ** -0.5, 1.0).astype(F32)
        for hh in range(PROJ_SUB // RET_DK):
            a = hh * RET_DK
            x1 = acc[:, a:a + half]
            x2 = acc[:, a + half:a + RET_DK]
            o_ref[0, :, lo + a:lo + a + half] = ((x1 * cos - x2 * sin) * scale).astype(BF16)
            o_ref[0, :, lo + a + half:lo + a + RET_DK] = ((x2 * cos + x1 * sin) * scale).astype(BF16)

    def chains(store):
        for s in range(PROJ_TN // PROJ_SUB):
            lo = s * PROJ_SUB
            store(_dot(hn_ref[...], w_ref[:, lo:lo + PROJ_SUB]), lo)

    @pl.when(j < n_plain)
    def _():
        chains(store_plain)

    @pl.when(jnp.logical_and(j >= n_plain, j < n_plain + n_silu))
    def _():
        chains(store_silu)

    @pl.when(j >= n_plain + n_silu)
    def _():
        chains(store_rope)


def _ret_proj(x, gain, w_vgqk, cos, sin):
    B, S, D = x.shape
    N = w_vgqk.shape[1]
    tm, tn = PROJ_TM, PROJ_TN
    half = RET_DK // 2
    kern = functools.partial(_ret_proj_kernel, n_plain=RET_HV // tn, n_silu=RET_HV // tn,
                             n_query=RET_HQ // tn)
    return pl.pallas_call(
        kern,
        grid=(B, S // tm, N // tn),
        in_specs=[
            pl.BlockSpec((1, tm, D), lambda b, i, j: (b, i, 0)),
            pl.BlockSpec((1, D), lambda b, i, j: (0, 0)),
            pl.BlockSpec((D, tn), lambda b, i, j: (0, j)),
            pl.BlockSpec((tm, half), lambda b, i, j: (i, 0)),
            pl.BlockSpec((tm, half), lambda b, i, j: (i, 0)),
        ],
        out_specs=pl.BlockSpec((1, tm, tn), lambda b, i, j: (b, i, j)),
        out_shape=jax.ShapeDtypeStruct((B, S, N), BF16),
        scratch_shapes=[pltpu.VMEM((tm, D), BF16)],
        compiler_params=_params("parallel", "parallel", "arbitrary"),
        name="ret_proj",
    )(x, gain, w_vgqk, cos, sin)


def _retention_kernel(lg_ref, q_ref, k_ref, v_ref, *rest, reverse, finish):
    if finish:
        yb_ref, gate_ref, gain_ref, o_ref = rest[:4]
        state_ref, dec_ref, qd_ref, kd_ref, cd_ref = rest[4:]
    else:
        o_ref = rest[0]
        state_ref, dec_ref, qd_ref, kd_ref, cd_ref = rest[1:]
    C = RET_BLOCK
    t = pl.program_id(1)
    last_dims = (((1,), (1,)), ((), ()))
    first_dims = (((0,), (0,)), ((), ()))

    @pl.when(t == 0)
    def _():
        state_ref[...] = jnp.zeros_like(state_ref)
        ri = lax.broadcasted_iota(jnp.int32, (C, C), 0)
        ci = lax.broadcasted_iota(jnp.int32, (C, C), 1)
        d = (ci - ri) if reverse else (ri - ci)
        mask = (d > 0) if reverse else (d >= 0)
        dist = jnp.maximum(d, 0).astype(F32)
        qrow = lax.broadcasted_iota(jnp.int32, (C, RET_DV), 0).astype(F32)
        krow = lax.broadcasted_iota(jnp.int32, (C, RET_DK), 0).astype(F32)
        for h in range(RET_HEADS):
            lg = lg_ref[h]
            dec_ref[h] = jnp.where(mask, jnp.exp(lg * dist), 0.0)
            if reverse:
                qd_ref[h] = jnp.exp(lg * (C - qrow))
                kd_ref[h] = jnp.exp(lg * krow)
            else:
                qd_ref[h] = jnp.exp(lg * (qrow + 1.0))
                kd_ref[h] = jnp.exp(lg * (C - 1.0 - krow))
            cd_ref[h] = jnp.exp(jnp.full((1, RET_DV), lg * C, F32))

    for h in range(RET_HEADS):
        kcols = slice(h * RET_DK, (h + 1) * RET_DK)
        vcols = slice(h * RET_DV, (h + 1) * RET_DV)
        q = q_ref[0, :, kcols]
        k = k_ref[0, :, kcols]
        v = v_ref[0, :, vcols]
        scores = lax.dot_general(q, k, last_dims, preferred_element_type=F32) * dec_ref[h]
        y = _dot(scores.astype(BF16), v)
        y = y + qd_ref[h] * _dot(q, state_ref[h].astype(BF16))
        ks = (k.astype(F32) * kd_ref[h]).astype(BF16)
        state_ref[h] = (state_ref[h] * cd_ref[h]
                        + lax.dot_general(ks, v, first_dims, preferred_element_type=F32))
        if finish:
            y = y + yb_ref[0, :, vcols].astype(F32)
            mu = jnp.mean(y, axis=-1, keepdims=True)
            yc = y - mu
            var = jnp.mean(yc * yc, axis=-1, keepdims=True)
            yn = yc * lax.rsqrt(var + LN_EPS) * gain_ref[:, vcols]
            o_ref[0, :, vcols] = (gate_ref[0, :, vcols].astype(F32) * yn).astype(BF16)
        else:
            o_ref[0, :, vcols] = y.astype(BF16)


def _retention(vgqk, log_gamma, *, reverse, y_other=None, gn_gain=None):
    B, S, _ = vgqk.shape
    C = RET_BLOCK
    H = RET_HEADS
    nC = S // C
    finish = y_other is not None
    q_blk = 2 * RET_HV // RET_HQ
    k_blk = q_blk + 1
    if reverse:
        cidx = lambda t: nC - 1 - t
    else:
        cidx = lambda t: t
    in_specs = [
        pl.BlockSpec(memory_space=pltpu.SMEM),
        pl.BlockSpec((1, C, RET_HQ), lambda b, t: (b, cidx(t), q_blk)),
        pl.BlockSpec((1, C, RET_HQ), lambda b, t: (b, cidx(t), k_blk)),
        pl.BlockSpec((1, C, RET_HV), lambda b, t: (b, cidx(t), 0)),
    ]
    args = [log_gamma, vgqk, vgqk, vgqk]
    if finish:
        in_specs += [
            pl.BlockSpec((1, C, RET_HV), lambda b, t: (b, cidx(t), 0)),
            pl.BlockSpec((1, C, RET_HV), lambda b, t: (b, cidx(t), 1)),
            pl.BlockSpec((1, RET_HV), lambda b, t: (0, 0)),
        ]
        args += [y_other, vgqk, gn_gain]
    kern = functools.partial(_retention_kernel, reverse=reverse, finish=finish)
    return pl.pallas_call(
        kern,
        grid=(B, nC),
        in_specs=in_specs,
        out_specs=pl.BlockSpec((1, C, RET_HV), lambda b, t: (b, cidx(t), 0)),
        out_shape=jax.ShapeDtypeStruct((B, S, RET_HV), BF16),
        scratch_shapes=[
            pltpu.VMEM((H, RET_DK, RET_DV), F32),
            pltpu.VMEM((H, C, C), F32),
            pltpu.VMEM((H, C, RET_DV), F32),
            pltpu.VMEM((H, C, RET_DK), F32),
            pltpu.VMEM((H, 1, RET_DV), F32),
        ],
        compiler_params=_params("parallel", "arbitrary"),
        name="retention_rev" if reverse else "retention_fwd",
    )(*args)


def _out_proj_kernel(a_ref, w_ref, x_ref, o_ref):
    N = w_ref.shape[1]
    for s in range(N // PROJ_SUB):
        cols = slice(s * PROJ_SUB, (s + 1) * PROJ_SUB)
        o_ref[0, :, cols] = x_ref[0, :, cols] + _dot(a_ref[0], w_ref[:, cols])


def _out_proj(a, w, x):
    B, S, K = a.shape
    N = w.shape[1]
    tm = OUT_TM
    return pl.pallas_call(
        _out_proj_kernel,
        grid=(B, S // tm),
        in_specs=[
            pl.BlockSpec((1, tm, K), lambda b, i: (b, i, 0)),
            pl.BlockSpec((K, N), lambda b, i: (0, 0), pipeline_mode=pl.Buffered(1)),
            pl.BlockSpec((1, tm, N), lambda b, i: (b, i, 0)),
        ],
        out_specs=pl.BlockSpec((1, tm, N), lambda b, i: (b, i, 0)),
        out_shape=jax.ShapeDtypeStruct((B, S, N), F32),
        compiler_params=_params("parallel", "parallel"),
        name="out_proj",
    )(a, w, x)


def _sgu_in_kernel(x_ref, gain_ref, w_ref, b_ref, z_ref, mean_ref, rstd_ref,
                   hn_ref, s1_ref, s2_ref, *, n_u, n_tiles):
    j = pl.program_id(2)

    @pl.when(j == 0)
    def _():
        hn_ref[...] = _rms_rows(x_ref[0], gain_ref[...]).astype(BF16)
        s1_ref[...] = jnp.zeros_like(s1_ref)
        s2_ref[...] = jnp.zeros_like(s2_ref)

    def chains(with_stats):
        for s in range(PROJ_TN // PROJ_SUB):
            cols = slice(s * PROJ_SUB, (s + 1) * PROJ_SUB)
            acc = _dot(hn_ref[...], w_ref[:, cols]) + b_ref[:, cols]
            z = 0.5 * acc * (1.0 + lax.erf(acc * np.float32(np.sqrt(0.5))))
            z_ref[0, :, cols] = z.astype(BF16)
            if with_stats:
                s1_ref[...] += jnp.sum(z, axis=-1, keepdims=True)
                s2_ref[...] += jnp.sum(z * z, axis=-1, keepdims=True)

    @pl.when(j < n_u)
    def _():
        chains(False)

    @pl.when(j >= n_u)
    def _():
        chains(True)

    @pl.when(j == n_tiles - 1)
    def _():
        mean = s1_ref[...] * (1.0 / SGU_E)
        var = s2_ref[...] * (1.0 / SGU_E) - mean * mean
        mean_ref[0] = mean
        rstd_ref[0] = lax.rsqrt(var + LN_EPS)


def _sgu_in(x, gain, w, b):
    B, S, D = x.shape
    N = w.shape[1]
    tm, tn = PROJ_TM, PROJ_TN
    n_tiles = N // tn
    kern = functools.partial(_sgu_in_kernel, n_u=SGU_E // tn, n_tiles=n_tiles)
    return pl.pallas_call(
        kern,
        grid=(B, S // tm, n_tiles),
        in_specs=[
            pl.BlockSpec((1, tm, D), lambda b_, i, j: (b_, i, 0)),
            pl.BlockSpec((1, D), lambda b_, i, j: (0, 0)),
            pl.BlockSpec((D, tn), lambda b_, i, j: (0, j)),
            pl.BlockSpec((1, tn), lambda b_, i, j: (0, j)),
        ],
        out_specs=[
            pl.BlockSpec((1, tm, tn), lambda b_, i, j: (b_, i, j)),
            pl.BlockSpec((1, tm, 1), lambda b_, i, j: (b_, i, 0)),
            pl.BlockSpec((1, tm, 1), lambda b_, i, j: (b_, i, 0)),
        ],
        out_shape=[
            jax.ShapeDtypeStruct((B, S, N), BF16),
            jax.ShapeDtypeStruct((B, S, 1), F32),
            jax.ShapeDtypeStruct((B, S, 1), F32),
        ],
        scratch_shapes=[
            pltpu.VMEM((tm, D), BF16),
            pltpu.VMEM((tm, 1), F32),
            pltpu.VMEM((tm, 1), F32),
        ],
        compiler_params=_params("parallel", "parallel", "arbitrary"),
        name="sgu_in",
    )(x, gain, w, b)


def _sgu_out_kernel(u_ref, v_ref, mean_ref, rstd_ref, lng_ref, lnb_ref, ws_ref,
                    bs_ref, wo_ref, wo_tail_ref, x_ref, o_ref, acc_ref, ga_ref, gb_ref,
                    *, n_pairs):
    s = pl.program_id(2)
    gw = SGU_GW

    @pl.when(s == 0)
    def _():
        acc_ref[...] = jnp.zeros_like(acc_ref)
        gb_ref[...] = jnp.zeros_like(gb_ref)

    def gate_group(k, out_ref):
        g = 2 * s + k
        cols = slice(k * gw, (k + 1) * gw)
        vn = ((v_ref[0, :, cols].astype(F32) - mean_ref[0]) * rstd_ref[0] * lng_ref[g]
              + lnb_ref[g]).astype(BF16)
        ws = ws_ref[g]
        bs = bs_ref[g]
        for c in range(SGU_TM // SGU_CHUNK):
            rows = slice(c * SGU_CHUNK, (c + 1) * SGU_CHUNK)
            mixed = _dot(ws, vn[rows]) + bs
            out_ref[rows, :] = (u_ref[0, rows, cols].astype(F32) * mixed).astype(BF16)

    acc_ref[...] += _dot(gb_ref[...], wo_ref[0:gw, :])
    gate_group(0, ga_ref)
    acc_ref[...] += _dot(ga_ref[...], wo_ref[gw:2 * gw, :])
    gate_group(1, gb_ref)

    @pl.when(s == n_pairs - 1)
    def _():
        o_ref[0] = x_ref[0] + acc_ref[...] + _dot(gb_ref[...], wo_tail_ref[...])


def _sgu_out(z, mean, rstd, ln_g, ln_b, w_s, b_s, w_out_shifted, x):
    B, S, D = x.shape
    tm = SGU_TM
    G = SGU_GROUPS
    gw = SGU_GW
    n_pairs = G // 2
    kern = functools.partial(_sgu_out_kernel, n_pairs=n_pairs)
    whole = lambda a: pl.BlockSpec(a.shape, lambda b, i, s: (0,) * a.ndim)
    return pl.pallas_call(
        kern,
        grid=(B, S // tm, n_pairs),
        in_specs=[
            pl.BlockSpec((1, tm, 2 * gw), lambda b, i, s: (b, i, s)),
            pl.BlockSpec((1, tm, 2 * gw), lambda b, i, s: (b, i, n_pairs + s)),
            pl.BlockSpec((1, tm, 1), lambda b, i, s: (b, i, 0)),
            pl.BlockSpec((1, tm, 1), lambda b, i, s: (b, i, 0)),
            whole(ln_g),
            whole(ln_b),
            whole(w_s),
            whole(b_s),
            pl.BlockSpec((2 * gw, D), lambda b, i, s: (s, 0)),
            pl.BlockSpec((gw, D), lambda b, i, s: (G, 0)),
            pl.BlockSpec((1, tm, D), lambda b, i, s: (b, i, 0)),
        ],
        out_specs=pl.BlockSpec((1, tm, D), lambda b, i, s: (b, i, 0)),
        out_shape=jax.ShapeDtypeStruct((B, S, D), F32),
        scratch_shapes=[
            pltpu.VMEM((tm, D), F32),
            pltpu.VMEM((tm, gw), BF16),
            pltpu.VMEM((tm, gw), BF16),
        ],
        compiler_params=_params("parallel", "parallel", "arbitrary"),
        name="sgu_out",
    )(z, z, mean, rstd, ln_g, ln_b, w_s, b_s, w_out_shifted, w_out_shifted, x)


def _stage(a, scr_ref, first):
    tm = FFN_TM
    for c in range(FFN_HALF // LANES):
        blk = a[:, c * LANES:(c + 1) * LANES]
        scr_ref[first + c, 0:HALO, :] = blk[tm + HALO:tm + 2 * HALO]
        scr_ref[first + c, HALO:tm + 2 * HALO, :] = blk[:tm + HALO]


def _conv3(scr_ref, slab, cw, cb):
    tm = FFN_TM
    prev = scr_ref[slab, HALO - 1:HALO - 1 + tm, :]
    cur = scr_ref[slab, HALO:HALO + tm, :]
    nxt = scr_ref[slab, HALO + 1:HALO + 1 + tm, :]
    return prev * cw[0:1, :] + cur * cw[1:2, :] + nxt * cw[2:3, :] + cb


def _conv_ffn_kernel(x_ref, xprev_ref, xnext_ref, gain_ref, wg_ref, wu_ref, cw_ref,
                     wo_ref, fin_ref, o_ref, hn_ref, acta_ref, actb_ref, sa_ref, sb_ref,
                     *, n_row_tiles, n_f, final_norm):
    i = pl.program_id(1)
    j = pl.program_id(2)
    tm = FFN_TM
    nblk = FFN_HALF // LANES
    up0 = FFN_TF

    @pl.when(j == 0)
    def _():
        gain = gain_ref[...]
        hn_ref[0:tm, :] = _rms_rows(x_ref[0], gain).astype(BF16)
        has_next = (i < n_row_tiles - 1).astype(F32)
        has_prev = (i > 0).astype(F32)
        halo = jnp.concatenate([_rms_rows(xnext_ref[0], gain) * has_next,
                                _rms_rows(xprev_ref[0], gain) * has_prev], axis=0)
        hn_ref[tm:tm + 2 * HALO, :] = halo.astype(BF16)
        o_ref[0] = x_ref[0]

    def in_proj(lo, scr_ref):
        hn = hn_ref[...]
        _stage(_dot(hn, wg_ref[:, lo:lo + FFN_HALF]), scr_ref, 0)
        _stage(_dot(hn, wu_ref[:, lo:lo + FFN_HALF]), scr_ref, nblk)

    def conv_act(scr_ref, lo, act_ref):
        cw = cw_ref[j]
        for c in range(nblk):
            g0 = lo + c * LANES
            u0 = up0 + g0
            gate = _conv3(scr_ref, c, cw[0:3, g0:g0 + LANES], cw[3:4, g0:g0 + LANES])
            up = _conv3(scr_ref, nblk + c, cw[0:3, u0:u0 + LANES], cw[3:4, u0:u0 + LANES])
            act_ref[:, c * LANES:(c + 1) * LANES] = (_silu(gate) * up).astype(BF16)

    in_proj(0, sa_ref)
    in_proj(FFN_HALF, sb_ref)
    conv_act(sa_ref, 0, acta_ref)
    o_ref[0] += _dot(acta_ref[...], wo_ref[0:FFN_HALF, :])
    conv_act(sb_ref, FFN_HALF, actb_ref)
    o_ref[0] += _dot(actb_ref[...], wo_ref[FFN_HALF:FFN_TF, :])

    if final_norm:
        @pl.when(j == n_f - 1)
        def _():
            o_ref[0] = _rms_rows(o_ref[0], fin_ref[...])


def _conv_ffn(x, gain, w_in, conv_tiles, w_out, fin_gain, *, final_norm):
    B, S, D = x.shape
    tm, tf = FFN_TM, FFN_TF
    n_f = FFN_DIM // tf
    n_row_tiles = S // tm
    hb = tm // HALO
    last_hb = S // HALO - 1
    kern = functools.partial(_conv_ffn_kernel, n_row_tiles=n_row_tiles, n_f=n_f,
                             final_norm=final_norm)
    return pl.pallas_call(
        kern,
        grid=(B, n_row_tiles, n_f),
        in_specs=[
            pl.BlockSpec((1, tm, D), lambda b, i, j: (b, i, 0),
                         pipeline_mode=pl.Buffered(1)),
            pl.BlockSpec((1, HALO, D),
                         lambda b, i, j: (b, jnp.maximum(i * hb - 1, 0), 0)),
            pl.BlockSpec((1, HALO, D),
                         lambda b, i, j: (b, jnp.minimum((i + 1) * hb, last_hb), 0)),
            pl.BlockSpec((1, D), lambda b, i, j: (0, 0)),
            pl.BlockSpec((D, tf), lambda b, i, j: (0, j)),
            pl.BlockSpec((D, tf), lambda b, i, j: (0, n_f + j)),
            pl.BlockSpec(conv_tiles.shape, lambda b, i, j: (0, 0, 0)),
            pl.BlockSpec((tf, D), lambda b, i, j: (j, 0)),
            pl.BlockSpec((1, D), lambda b, i, j: (0, 0)),
        ],
        out_specs=pl.BlockSpec((1, tm, D), lambda b, i, j: (b, i, 0)),
        out_shape=jax.ShapeDtypeStruct((B, S, D), F32),
        scratch_shapes=[
            pltpu.VMEM((tm + 2 * HALO, D), BF16),
            pltpu.VMEM((tm, FFN_HALF), BF16),
            pltpu.VMEM((tm, FFN_HALF), BF16),
            pltpu.VMEM((2 * FFN_HALF // LANES, tm + 2 * HALO, LANES), F32),
            pltpu.VMEM((2 * FFN_HALF // LANES, tm + 2 * HALO, LANES), F32),
        ],
        compiler_params=_params("parallel", "parallel", "arbitrary"),
        name="conv_ffn",
    )(x, x, x, gain, w_in, w_in, conv_tiles, w_out, fin_gain)


def _conv_tiles(conv_w, conv_b):
    n_f = FFN_DIM // FFN_TF
    a = jnp.concatenate([conv_w, conv_b.reshape(1, -1)], axis=0)
    a = a.reshape(4, 2, n_f, FFN_TF)
    return a.transpose(2, 0, 1, 3).reshape(n_f, 4, 2 * FFN_TF)


def _rope_tables(S):
    half = RET_DK // 2
    inv = ROPE_BASE ** (-jnp.arange(half, dtype=F32) / half)
    ang = jnp.arange(S, dtype=F32)[:, None] * inv[None, :]
    return jnp.cos(ang), jnp.sin(ang)


def _row(v):
    return v.reshape(1, -1)


def _prepare(ln_mix, ln_ffn, ret_w_in, ret_log_decay_fwd, ret_log_decay_bwd,
             ret_gn_gain, ret_w_out, sgu_w_in, sgu_b_in, sgu_ln_g, sgu_ln_b,
             sgu_w_s, sgu_b_s, sgu_w_out, ffn_w_in, ffn_conv_w, ffn_conv_b,
             ffn_w_out, ln_final):
    depth = ln_mix.shape[0]
    hq = RET_HQ
    layers = []
    for i in range(depth):
        j = i // 2
        p = {"ln_mix": _row(ln_mix[i]), "ln_ffn": _row(ln_ffn[i])}
        if i % 2 == 0:
            w = ret_w_in[j]
            p["w_vgqk"] = jnp.concatenate([w[:, 2 * hq:], w[:, :2 * hq]], axis=1).astype(BF16)
            p["lg_fwd"] = ret_log_decay_fwd[j]
            p["lg_bwd"] = ret_log_decay_bwd[j]
            p["gn_gain"] = _row(ret_gn_gain[j])
            p["w_out"] = ret_w_out[j].astype(BF16)
        else:
            p["w_in"] = sgu_w_in[j].astype(BF16)
            p["b_in"] = _row(sgu_b_in[j])
            p["ln_g"] = sgu_ln_g[j].reshape(SGU_GROUPS, 1, SGU_GW)
            p["ln_b"] = sgu_ln_b[j].reshape(SGU_GROUPS, 1, SGU_GW)
            p["w_s"] = sgu_w_s[j].astype(BF16)
            p["b_s"] = sgu_b_s[j][:, :, None]
            p["w_out"] = jnp.pad(sgu_w_out[j].astype(BF16), ((SGU_GW, 0), (0, 0)))
        p["ffn_w_in"] = ffn_w_in[i].astype(BF16)
        p["ffn_conv"] = _conv_tiles(ffn_conv_w[i], ffn_conv_b[i])
        p["ffn_w_out"] = ffn_w_out[i].astype(BF16)
        layers.append(p)
    return layers, _row(ln_final)


def _trunk(x, layers, fin_gain, cos, sin):
    for i, p in enumerate(layers):
        if i % 2 == 0:
            vgqk = _ret_proj(x, p["ln_mix"], p["w_vgqk"], cos, sin)
            y_rev = _retention(vgqk, p["lg_bwd"], reverse=True)
            a = _retention(vgqk, p["lg_fwd"], reverse=False, y_other=y_rev,
                           gn_gain=p["gn_gain"])
            x = _out_proj(a, p["w_out"], x)
        else:
            z, mean, rstd = _sgu_in(x, p["ln_mix"], p["w_in"], p["b_in"])
            x = _sgu_out(z, mean, rstd, p["ln_g"], p["ln_b"], p["w_s"], p["b_s"],
                         p["w_out"], x)
        x = _conv_ffn(x, p["ln_ffn"], p["ffn_w_in"], p["ffn_conv"], p["ffn_w_out"],
                      fin_gain, final_norm=(i == len(layers) - 1))
    return x


def kernel(x_prompt, x_sample, ln_mix, ln_ffn, ret_w_in, ret_log_decay_fwd, ret_log_decay_bwd, ret_gn_gain, ret_w_out, sgu_w_in, sgu_b_in, sgu_ln_g, sgu_ln_b, sgu_w_s, sgu_b_s, sgu_w_out, ffn_w_in, ffn_conv_w, ffn_conv_b, ffn_w_out, ln_final):
    layers, fin_gain = _prepare(
        ln_mix, ln_ffn, ret_w_in, ret_log_decay_fwd, ret_log_decay_bwd, ret_gn_gain,
        ret_w_out, sgu_w_in, sgu_b_in, sgu_ln_g, sgu_ln_b, sgu_w_s, sgu_b_s,
        sgu_w_out, ffn_w_in, ffn_conv_w, ffn_conv_b, ffn_w_out, ln_final)
    cos, sin = _rope_tables(max(x_prompt.shape[1], x_sample.shape[1]))
    return (_trunk(x_prompt, layers, fin_gain, cos, sin),
            _trunk(x_sample, layers, fin_gain, cos, sin))
```

```python
import functools

import jax
import jax.numpy as jnp
import numpy as np
from jax import lax
from jax.experimental import pallas as pl
from jax.experimental.pallas import tpu as pltpu

D_MODEL = 2048
RET_HEADS = 8
RET_DK = D_MODEL // RET_HEADS
RET_DV = 2 * D_MODEL // RET_HEADS
RET_HQ = RET_HEADS * RET_DK
RET_HV = RET_HEADS * RET_DV
ROPE_BASE = 10000.0
SGU_E = 3 * D_MODEL
SGU_GROUPS = 8
SGU_CHUNK = 128
SGU_GW = SGU_E // SGU_GROUPS
FFN_DIM = 5632
EPS = 1e-6
LN_EPS = 1e-5

SUBLANES_F32 = 8
LANES = 128
VMEM_LIMIT_BYTES = 60 * 1024 * 1024

PROJ_TM = 1024
PROJ_TN = 2048
PROJ_SUB = 512
RET_BLOCK = 256
OUT_TM = 512
SGU_TM = 512
FFN_TM = 1024
FFN_TF = 512
FFN_HALF = FFN_TF // 2
HALO = SUBLANES_F32

BF16 = jnp.bfloat16
F32 = jnp.float32


def _params(*semantics):
    return pltpu.CompilerParams(dimension_semantics=semantics,
                                vmem_limit_bytes=VMEM_LIMIT_BYTES)


def _rms_rows(x, gain):
    ms = jnp.mean(x * x, axis=-1, keepdims=True)
    return x * lax.rsqrt(ms + EPS) * gain


def _dot(a, b):
    return jnp.dot(a, b, preferred_element_type=F32)


def _silu(a):
    return a * jax.nn.sigmoid(a)


def _ret_proj_kernel(x_ref, gain_ref, w_ref, cos_ref, sin_ref, o_ref, hn_ref,
                     *, n_plain, n_silu, n_query):
    j = pl.program_id(2)

    @pl.when(j == 0)
    def _():
        hn_ref[...] = _rms_rows(x_ref[0], gain_ref[...]).astype(BF16)

    def store_plain(acc, lo):
        o_ref[0, :, lo:lo + PROJ_SUB] = acc.astype(BF16)

    def store_silu(acc, lo):
        o_ref[0, :, lo:lo + PROJ_SUB] = _silu(acc).astype(BF16)

    def store_rope(acc, lo):
        cos = cos_ref[...]
        sin = sin_ref[...]
        half = RET_DK // 2
        scale = jnp.where(j >= n_plain + n_silu + n_query, RET_DK ** -0.5, 1.0).astype(F32)
        for hh in range(PROJ_SUB // RET_DK):
            a = hh * RET_DK
            x1 = acc[:, a:a + half]
            x2 = acc[:, a + half:a + RET_DK]
            o_ref[0, :, lo + a:lo + a + half] = ((x1 * cos - x2 * sin) * scale).astype(BF16)
            o_ref[0, :, lo + a + half:lo + a + RET_DK] = ((x2 * cos + x1 * sin) * scale).astype(BF16)

    def chains(store):
        for s in range(PROJ_TN // PROJ_SUB):
            lo = s * PROJ_SUB
            store(_dot(hn_ref[...], w_ref[:, lo:lo + PROJ_SUB]), lo)

    @pl.when(j < n_plain)
    def _():
        chains(store_plain)

    @pl.when(jnp.logical_and(j >= n_plain, j < n_plain + n_silu))
    def _():
        chains(store_silu)

    @pl.when(j >= n_plain + n_silu)
    def _():
        chains(store_rope)


def _ret_proj(x, gain, w_vgqk, cos, sin):
    B, S, D = x.shape
    N = w_vgqk.shape[1]
    tm, tn = PROJ_TM, PROJ_TN
    half = RET_DK // 2
    kern = functools.partial(_ret_proj_kernel, n_plain=RET_HV // tn, n_silu=RET_HV // tn,
                             n_query=RET_HQ // tn)
    return pl.pallas_call(
        kern,
        grid=(B, S // tm, N // tn),
        in_specs=[
            pl.BlockSpec((1, tm, D), lambda b, i, j: (b, i, 0)),
            pl.BlockSpec((1, D), lambda b, i, j: (0, 0)),
            pl.BlockSpec((D, tn), lambda b, i, j: (0, j)),
            pl.BlockSpec((tm, half), lambda b, i, j: (i, 0)),
            pl.BlockSpec((tm, half), lambda b, i, j: (i, 0)),
        ],
        out_specs=pl.BlockSpec((1, tm, tn), lambda b, i, j: (b, i, j)),
        out_shape=jax.ShapeDtypeStruct((B, S, N), BF16),
        scratch_shapes=[pltpu.VMEM((tm, D), BF16)],
        compiler_params=_params("parallel", "parallel", "arbitrary"),
        name="ret_proj",
    )(x, gain, w_vgqk, cos, sin)


def _retention_kernel(lg_ref, q_ref, k_ref, v_ref, *rest, reverse, finish):
    if finish:
        yb_ref, gate_ref, gain_ref, o_ref = rest[:4]
        state_ref, dec_ref, qd_ref, kd_ref, cd_ref = rest[4:]
    else:
        o_ref = rest[0]
        state_ref, dec_ref, qd_ref, kd_ref, cd_ref = rest[1:]
    C = RET_BLOCK
    t = pl.program_id(1)
    last_dims = (((1,), (1,)), ((), ()))
    first_dims = (((0,), (0,)), ((), ()))

    @pl.when(t == 0)
    def _():
        state_ref[...] = jnp.zeros_like(state_ref)
        ri = lax.broadcasted_iota(jnp.int32, (C, C), 0)
        ci = lax.broadcasted_iota(jnp.int32, (C, C), 1)
        d = (ci - ri) if reverse else (ri - ci)
        mask = (d > 0) if reverse else (d >= 0)
        dist = jnp.maximum(d, 0).astype(F32)
        qrow = lax.broadcasted_iota(jnp.int32, (C, RET_DV), 0).astype(F32)
        krow = lax.broadcasted_iota(jnp.int32, (C, RET_DK), 0).astype(F32)
        for h in range(RET_HEADS):
            lg = lg_ref[h]
            dec_ref[h] = jnp.where(mask, jnp.exp(lg * dist), 0.0)
            if reverse:
                qd_ref[h] = jnp.exp(lg * (C - qrow))
                kd_ref[h] = jnp.exp(lg * krow)
            else:
                qd_ref[h] = jnp.exp(lg * (qrow + 1.0))
                kd_ref[h] = jnp.exp(lg * (C - 1.0 - krow))
            cd_ref[h] = jnp.exp(jnp.full((1, RET_DV), lg * C, F32))

    for h in range(RET_HEADS):
        kcols = slice(h * RET_DK, (h + 1) * RET_DK)
        vcols = slice(h * RET_DV, (h + 1) * RET_DV)
        q = q_ref[0, :, kcols]
        k = k_ref[0, :, kcols]
        v = v_ref[0, :, vcols]
        scores = lax.dot_general(q, k, last_dims, preferred_element_type=F32) * dec_ref[h]
        y = _dot(scores.astype(BF16), v)
        y = y + qd_ref[h] * _dot(q, state_ref[h].astype(BF16))
        ks = (k.astype(F32) * kd_ref[h]).astype(BF16)
        state_ref[h] = (state_ref[h] * cd_ref[h]
                        + lax.dot_general(ks, v, first_dims, preferred_element_type=F32))
        if finish:
            y = y + yb_ref[0, :, vcols].astype(F32)
            mu = jnp.mean(y, axis=-1, keepdims=True)
            yc = y - mu
            var = jnp.mean(yc * yc, axis=-1, keepdims=True)
            yn = yc * lax.rsqrt(var + LN_EPS) * gain_ref[:, vcols]
            o_ref[0, :, vcols] = (gate_ref[0, :, vcols].astype(F32) * yn).astype(BF16)
        else:
            o_ref[0, :, vcols] = y.astype(BF16)


def _retention(vgqk, log_gamma, *, reverse, y_other=None, gn_gain=None):
    B, S, _ = vgqk.shape
    C = RET_BLOCK
    H = RET_HEADS
    nC = S // C
    finish = y_other is not None
    q_blk = 2 * RET_HV // RET_HQ
    k_blk = q_blk + 1
    if reverse:
        cidx = lambda t: nC - 1 - t
    else:
        cidx = lambda t: t
    in_specs = [
        pl.BlockSpec(memory_space=pltpu.SMEM),
        pl.BlockSpec((1, C, RET_HQ), lambda b, t: (b, cidx(t), q_blk)),
        pl.BlockSpec((1, C, RET_HQ), lambda b, t: (b, cidx(t), k_blk)),
        pl.BlockSpec((1, C, RET_HV), lambda b, t: (b, cidx(t), 0)),
    ]
    args = [log_gamma, vgqk, vgqk, vgqk]
    if finish:
        in_specs += [
            pl.BlockSpec((1, C, RET_HV), lambda b, t: (b, cidx(t), 0)),
            pl.BlockSpec((1, C, RET_HV), lambda b, t: (b, cidx(t), 1)),
            pl.BlockSpec((1, RET_HV), lambda b, t: (0, 0)),
        ]
        args += [y_other, vgqk, gn_gain]
    kern = functools.partial(_retention_kernel, reverse=reverse, finish=finish)
    return pl.pallas_call(
        kern,
        grid=(B, nC),
        in_specs=in_specs,
        out_specs=pl.BlockSpec((1, C, RET_HV), lambda b, t: (b, cidx(t), 0)),
        out_shape=jax.ShapeDtypeStruct((B, S, RET_HV), BF16),
        scratch_shapes=[
            pltpu.VMEM((H, RET_DK, RET_DV), F32),
            pltpu.VMEM((H, C, C), F32),
            pltpu.VMEM((H, C, RET_DV), F32),
            pltpu.VMEM((H, C, RET_DK), F32),
            pltpu.VMEM((H, 1, RET_DV), F32),
        ],
        compiler_params=_params("parallel", "arbitrary"),
        name="retention_rev" if reverse else "retention_fwd",
    )(*args)


def _out_proj_kernel(a_ref, w_ref, x_ref, o_ref):
    N = w_ref.shape[1]
    for s in range(N // PROJ_SUB):
        cols = slice(s * PROJ_SUB, (s + 1) * PROJ_SUB)
        o_ref[0, :, cols] = x_ref[0, :, cols] + _dot(a_ref[0], w_ref[:, cols])


def _out_proj(a, w, x):
    B, S, K = a.shape
    N = w.shape[1]
    tm = OUT_TM
    return pl.pallas_call(
        _out_proj_kernel,
        grid=(B, S // tm),
        in_specs=[
            pl.BlockSpec((1, tm, K), lambda b, i: (b, i, 0)),
            pl.BlockSpec((K, N), lambda b, i: (0, 0), pipeline_mode=pl.Buffered(1)),
            pl.BlockSpec((1, tm, N), lambda b, i: (b, i, 0)),
        ],
        out_specs=pl.BlockSpec((1, tm, N), lambda b, i: (b, i, 0)),
        out_shape=jax.ShapeDtypeStruct((B, S, N), F32),
        compiler_params=_params("parallel", "parallel"),
        name="out_proj",
    )(a, w, x)


def _sgu_in_kernel(x_ref, gain_ref, w_ref, b_ref, z_ref, mean_ref, rstd_ref,
                   hn_ref, s1_ref, s2_ref, *, n_u, n_tiles):
    j = pl.program_id(2)

    @pl.when(j == 0)
    def _():
        hn_ref[...] = _rms_rows(x_ref[0], gain_ref[...]).astype(BF16)
        s1_ref[...] = jnp.zeros_like(s1_ref)
        s2_ref[...] = jnp.zeros_like(s2_ref)

    def chains(with_stats):
        for s in range(PROJ_TN // PROJ_SUB):
            cols = slice(s * PROJ_SUB, (s + 1) * PROJ_SUB)
            acc = _dot(hn_ref[...], w_ref[:, cols]) + b_ref[:, cols]
            z = 0.5 * acc * (1.0 + lax.erf(acc * np.float32(np.sqrt(0.5))))
            z_ref[0, :, cols] = z.astype(BF16)
            if with_stats:
                zz = z * z
                p1 = z[:, 0:LANES]
                p2 = zz[:, 0:LANES]
                for c in range(1, PROJ_SUB // LANES):
                    p1 = p1 + z[:, c * LANES:(c + 1) * LANES]
                    p2 = p2 + zz[:, c * LANES:(c + 1) * LANES]
                s1_ref[...] += p1
                s2_ref[...] += p2

    @pl.when(j < n_u)
    def _():
        chains(False)

    @pl.when(j >= n_u)
    def _():
        chains(True)

    @pl.when(j == n_tiles - 1)
    def _():
        mean = jnp.sum(s1_ref[...], axis=-1, keepdims=True) * (1.0 / SGU_E)
        var = jnp.sum(s2_ref[...], axis=-1, keepdims=True) * (1.0 / SGU_E) - mean * mean
        mean_ref[0] = mean
        rstd_ref[0] = lax.rsqrt(var + LN_EPS)


def _sgu_in(x, gain, w, b):
    B, S, D = x.shape
    N = w.shape[1]
    tm, tn = PROJ_TM, PROJ_TN
    n_tiles = N // tn
    kern = functools.partial(_sgu_in_kernel, n_u=SGU_E // tn, n_tiles=n_tiles)
    return pl.pallas_call(
        kern,
        grid=(B, S // tm, n_tiles),
        in_specs=[
            pl.BlockSpec((1, tm, D), lambda b_, i, j: (b_, i, 0)),
            pl.BlockSpec((1, D), lambda b_, i, j: (0, 0)),
            pl.BlockSpec((D, tn), lambda b_, i, j: (0, j)),
            pl.BlockSpec((1, tn), lambda b_, i, j: (0, j)),
        ],
        out_specs=[
            pl.BlockSpec((1, tm, tn), lambda b_, i, j: (b_, i, j)),
            pl.BlockSpec((1, tm, 1), lambda b_, i, j: (b_, i, 0)),
            pl.BlockSpec((1, tm, 1), lambda b_, i, j: (b_, i, 0)),
        ],
        out_shape=[
            jax.ShapeDtypeStruct((B, S, N), BF16),
            jax.ShapeDtypeStruct((B, S, 1), F32),
            jax.ShapeDtypeStruct((B, S, 1), F32),
        ],
        scratch_shapes=[
            pltpu.VMEM((tm, D), BF16),
            pltpu.VMEM((tm, LANES), F32),
            pltpu.VMEM((tm, LANES), F32),
        ],
        compiler_params=_params("parallel", "parallel", "arbitrary"),
        name="sgu_in",
    )(x, gain, w, b)


def _sgu_out_kernel(u_ref, v_ref, mean_ref, rstd_ref, lng_ref, lnb_ref, ws_ref,
                    bs_ref, wo_ref, wo_tail_ref, x_ref, o_ref, acc_ref, ga_ref, gb_ref,
                    *, n_pairs):
    s = pl.program_id(2)
    gw = SGU_GW

    @pl.when(s == 0)
    def _():
        acc_ref[...] = jnp.zeros_like(acc_ref)
        gb_ref[...] = jnp.zeros_like(gb_ref)

    def gate_group(k, out_ref):
        g = 2 * s + k
        cols = slice(k * gw, (k + 1) * gw)
        vn = ((v_ref[0, :, cols].astype(F32) - mean_ref[0]) * rstd_ref[0] * lng_ref[g]
              + lnb_ref[g]).astype(BF16)
        ws = ws_ref[g]
        bs = bs_ref[g]
        for c in range(SGU_TM // SGU_CHUNK):
            rows = slice(c * SGU_CHUNK, (c + 1) * SGU_CHUNK)
            mixed = _dot(ws, vn[rows]) + bs
            out_ref[rows, :] = (u_ref[0, rows, cols].astype(F32) * mixed).astype(BF16)

    acc_ref[...] += _dot(gb_ref[...], wo_ref[0:gw, :])
    gate_group(0, ga_ref)
    acc_ref[...] += _dot(ga_ref[...], wo_ref[gw:2 * gw, :])
    gate_group(1, gb_ref)

    @pl.when(s == n_pairs - 1)
    def _():
        o_ref[0] = x_ref[0] + acc_ref[...] + _dot(gb_ref[...], wo_tail_ref[...])


def _sgu_out(z, mean, rstd, ln_g, ln_b, w_s, b_s, w_out_shifted, x):
    B, S, D = x.shape
    tm = SGU_TM
    G = SGU_GROUPS
    gw = SGU_GW
    n_pairs = G // 2
    kern = functools.partial(_sgu_out_kernel, n_pairs=n_pairs)
    whole = lambda a: pl.BlockSpec(a.shape, lambda b, i, s: (0,) * a.ndim)
    return pl.pallas_call(
        kern,
        grid=(B, S // tm, n_pairs),
        in_specs=[
            pl.BlockSpec((1, tm, 2 * gw), lambda b, i, s: (b, i, s)),
            pl.BlockSpec((1, tm, 2 * gw), lambda b, i, s: (b, i, n_pairs + s)),
            pl.BlockSpec((1, tm, 1), lambda b, i, s: (b, i, 0)),
            pl.BlockSpec((1, tm, 1), lambda b, i, s: (b, i, 0)),
            whole(ln_g),
            whole(ln_b),
            whole(w_s),
            whole(b_s),
            pl.BlockSpec((2 * gw, D), lambda b, i, s: (s, 0)),
            pl.BlockSpec((gw, D), lambda b, i, s: (G, 0)),
            pl.BlockSpec((1, tm, D), lambda b, i, s: (b, i, 0)),
        ],
        out_specs=pl.BlockSpec((1, tm, D), lambda b, i, s: (b, i, 0)),
        out_shape=jax.ShapeDtypeStruct((B, S, D), F32),
        scratch_shapes=[
            pltpu.VMEM((tm, D), F32),
            pltpu.VMEM((tm, gw), BF16),
            pltpu.VMEM((tm, gw), BF16),
        ],
        compiler_params=_params("parallel", "parallel", "arbitrary"),
        name="sgu_out",
    )(z, z, mean, rstd, ln_g, ln_b, w_s, b_s, w_out_shifted, w_out_shifted, x)


def _stage(a, scr_ref, first):
    tm = FFN_TM
    for c in range(FFN_HALF // LANES):
        blk = a[:, c * LANES:(c + 1) * LANES]
        scr_ref[first + c, 0:HALO, :] = blk[tm + HALO:tm + 2 * HALO]
        scr_ref[first + c, HALO:tm + 2 * HALO, :] = blk[:tm + HALO]


def _conv3(scr_ref, slab, cw, cb):
    tm = FFN_TM
    prev = scr_ref[slab, HALO - 1:HALO - 1 + tm, :]
    cur = scr_ref[slab, HALO:HALO + tm, :]
    nxt = scr_ref[slab, HALO + 1:HALO + 1 + tm, :]
    return prev * cw[0:1, :] + cur * cw[1:2, :] + nxt * cw[2:3, :] + cb


def _conv_ffn_kernel(x_ref, xprev_ref, xnext_ref, gain_ref, wg_ref, wu_ref, cw_ref,
                     wo_ref, fin_ref, o_ref, hn_ref, acta_ref, actb_ref, sa_ref, sb_ref,
                     *, n_row_tiles, n_f, final_norm):
    i = pl.program_id(1)
    j = pl.program_id(2)
    tm = FFN_TM
    nblk = FFN_HALF // LANES
    up0 = FFN_TF

    @pl.when(j == 0)
    def _():
        gain = gain_ref[...]
        hn_ref[0:tm, :] = _rms_rows(x_ref[0], gain).astype(BF16)
        has_next = (i < n_row_tiles - 1).astype(F32)
        has_prev = (i > 0).astype(F32)
        halo = jnp.concatenate([_rms_rows(xnext_ref[0], gain) * has_next,
                                _rms_rows(xprev_ref[0], gain) * has_prev], axis=0)
        hn_ref[tm:tm + 2 * HALO, :] = halo.astype(BF16)
        o_ref[0] = x_ref[0]

    def in_proj(lo, scr_ref):
        hn = hn_ref[...]
        _stage(_dot(hn, wg_ref[:, lo:lo + FFN_HALF]), scr_ref, 0)
        _stage(_dot(hn, wu_ref[:, lo:lo + FFN_HALF]), scr_ref, nblk)

    def conv_act(scr_ref, lo, act_ref):
        cw = cw_ref[j]
        for c in range(nblk):
            g0 = lo + c * LANES
            u0 = up0 + g0
            gate = _conv3(scr_ref, c, cw[0:3, g0:g0 + LANES], cw[3:4, g0:g0 + LANES])
            up = _conv3(scr_ref, nblk + c, cw[0:3, u0:u0 + LANES], cw[3:4, u0:u0 + LANES])
            act_ref[:, c * LANES:(c + 1) * LANES] = (_silu(gate) * up).astype(BF16)

    in_proj(0, sa_ref)
    in_proj(FFN_HALF, sb_ref)
    conv_act(sa_ref, 0, acta_ref)
    o_ref[0] += _dot(acta_ref[...], wo_ref[0:FFN_HALF, :])
    conv_act(sb_ref, FFN_HALF, actb_ref)
    o_ref[0] += _dot(actb_ref[...], wo_ref[FFN_HALF:FFN_TF, :])

    if final_norm:
        @pl.when(j == n_f - 1)
        def _():
            o_ref[0] = _rms_rows(o_ref[0], fin_ref[...])


def _conv_ffn(x, gain, w_in, conv_tiles, w_out, fin_gain, *, final_norm):
    B, S, D = x.shape
    tm, tf = FFN_TM, FFN_TF
    n_f = FFN_DIM // tf
    n_row_tiles = S // tm
    hb = tm // HALO
    last_hb = S // HALO - 1
    kern = functools.partial(_conv_ffn_kernel, n_row_tiles=n_row_tiles, n_f=n_f,
                             final_norm=final_norm)
    return pl.pallas_call(
        kern,
        grid=(B, n_row_tiles, n_f),
        in_specs=[
            pl.BlockSpec((1, tm, D), lambda b, i, j: (b, i, 0),
                         pipeline_mode=pl.Buffered(1)),
            pl.BlockSpec((1, HALO, D),
                         lambda b, i, j: (b, jnp.maximum(i * hb - 1, 0), 0)),
            pl.BlockSpec((1, HALO, D),
                         lambda b, i, j: (b, jnp.minimum((i + 1) * hb, last_hb), 0)),
            pl.BlockSpec((1, D), lambda b, i, j: (0, 0)),
            pl.BlockSpec((D, tf), lambda b, i, j: (0, j)),
            pl.BlockSpec((D, tf), lambda b, i, j: (0, n_f + j)),
            pl.BlockSpec(conv_tiles.shape, lambda b, i, j: (0, 0, 0)),
            pl.BlockSpec((tf, D), lambda b, i, j: (j, 0)),
            pl.BlockSpec((1, D), lambda b, i, j: (0, 0)),
        ],
        out_specs=pl.BlockSpec((1, tm, D), lambda b, i, j: (b, i, 0)),
        out_shape=jax.ShapeDtypeStruct((B, S, D), F32),
        scratch_shapes=[
            pltpu.VMEM((tm + 2 * HALO, D), BF16),
            pltpu.VMEM((tm, FFN_HALF), BF16),
            pltpu.VMEM((tm, FFN_HALF), BF16),
            pltpu.VMEM((2 * FFN_HALF // LANES, tm + 2 * HALO, LANES), F32),
            pltpu.VMEM((2 * FFN_HALF // LANES, tm + 2 * HALO, LANES), F32),
        ],
        compiler_params=_params("parallel", "parallel", "arbitrary"),
        name="conv_ffn",
    )(x, x, x, gain, w_in, w_in, conv_tiles, w_out, fin_gain)


def _conv_tiles(conv_w, conv_b):
    n_f = FFN_DIM // FFN_TF
    a = jnp.concatenate([conv_w, conv_b.reshape(1, -1)], axis=0)
    a = a.reshape(4, 2, n_f, FFN_TF)
    return a.transpose(2, 0, 1, 3).reshape(n_f, 4, 2 * FFN_TF)


def _rope_tables(S):
    half = RET_DK // 2
    inv = ROPE_BASE ** (-jnp.arange(half, dtype=F32) / half)
    ang = jnp.arange(S, dtype=F32)[:, None] * inv[None, :]
    return jnp.cos(ang), jnp.sin(ang)


def _row(v):
    return v.reshape(1, -1)


def _prepare(ln_mix, ln_ffn, ret_w_in, ret_log_decay_fwd, ret_log_decay_bwd,
             ret_gn_gain, ret_w_out, sgu_w_in, sgu_b_in, sgu_ln_g, sgu_ln_b,
             sgu_w_s, sgu_b_s, sgu_w_out, ffn_w_in, ffn_conv_w, ffn_conv_b,
             ffn_w_out, ln_final):
    depth = ln_mix.shape[0]
    hq = RET_HQ
    layers = []
    for i in range(depth):
        j = i // 2
        p = {"ln_mix": _row(ln_mix[i]), "ln_ffn": _row(ln_ffn[i])}
        if i % 2 == 0:
            w = ret_w_in[j]
            p["w_vgqk"] = jnp.concatenate([w[:, 2 * hq:], w[:, :2 * hq]], axis=1).astype(BF16)
            p["lg_fwd"] = ret_log_decay_fwd[j]
            p["lg_bwd"] = ret_log_decay_bwd[j]
            p["gn_gain"] = _row(ret_gn_gain[j])
            p["w_out"] = ret_w_out[j].astype(BF16)
        else:
            p["w_in"] = sgu_w_in[j].astype(BF16)
            p["b_in"] = _row(sgu_b_in[j])
            p["ln_g"] = sgu_ln_g[j].reshape(SGU_GROUPS, 1, SGU_GW)
            p["ln_b"] = sgu_ln_b[j].reshape(SGU_GROUPS, 1, SGU_GW)
            p["w_s"] = sgu_w_s[j].astype(BF16)
            p["b_s"] = sgu_b_s[j][:, :, None]
            p["w_out"] = jnp.pad(sgu_w_out[j].astype(BF16), ((SGU_GW, 0), (0, 0)))
        p["ffn_w_in"] = ffn_w_in[i].astype(BF16)
        p["ffn_conv"] = _conv_tiles(ffn_conv_w[i], ffn_conv_b[i])
        p["ffn_w_out"] = ffn_w_out[i].astype(BF16)
        layers.append(p)
    return layers, _row(ln_final)


def _trunk(x, layers, fin_gain, cos, sin):
    for i, p in enumerate(layers):
        if i % 2 == 0:
            vgqk = _ret_proj(x, p["ln_mix"], p["w_vgqk"], cos, sin)
            y_rev = _retention(vgqk, p["lg_bwd"], reverse=True)
            a = _retention(vgqk, p["lg_fwd"], reverse=False, y_other=y_rev,
                           gn_gain=p["gn_gain"])
            x = _out_proj(a, p["w_out"], x)
        else:
            z, mean, rstd = _sgu_in(x, p["ln_mix"], p["w_in"], p["b_in"])
            x = _sgu_out(z, mean, rstd, p["ln_g"], p["ln_b"], p["w_s"], p["b_s"],
                         p["w_out"], x)
        x = _conv_ffn(x, p["ln_ffn"], p["ffn_w_in"], p["ffn_conv"], p["ffn_w_out"],
                      fin_gain, final_norm=(i == len(layers) - 1))
    return x


def kernel(x_prompt, x_sample, ln_mix, ln_ffn, ret_w_in, ret_log_decay_fwd, ret_log_decay_bwd, ret_gn_gain, ret_w_out, sgu_w_in, sgu_b_in, sgu_ln_g, sgu_ln_b, sgu_w_s, sgu_b_s, sgu_w_out, ffn_w_in, ffn_conv_w, ffn_conv_b, ffn_w_out, ln_final):
    layers, fin_gain = _prepare(
        ln_mix, ln_ffn, ret_w_in, ret_log_decay_fwd, ret_log_decay_bwd, ret_gn_gain,
        ret_w_out, sgu_w_in, sgu_b_in, sgu_ln_g, sgu_ln_b, sgu_w_s, sgu_b_s,
        sgu_w_out, ffn_w_in, ffn_conv_w, ffn_conv_b, ffn_w_out, ln_final)
    cos, sin = _rope_tables(max(x_prompt.shape[1], x_sample.shape[1]))
    return (_trunk(x_prompt, layers, fin_gain, cos, sin),
            _trunk(x_sample, layers, fin_gain, cos, sin))
```

```python
import functools

import jax
import jax.numpy as jnp
import numpy as np
from jax import lax
from jax.experimental import pallas as pl
from jax.experimental.pallas import tpu as pltpu

D_MODEL = 2048
RET_HEADS = 8
RET_DK = D_MODEL // RET_HEADS
RET_DV = 2 * D_MODEL // RET_HEADS
RET_HQ = RET_HEADS * RET_DK
RET_HV = RET_HEADS * RET_DV
ROPE_BASE = 10000.0
SGU_E = 3 * D_MODEL
SGU_GROUPS = 8
SGU_CHUNK = 128
SGU_GW = SGU_E // SGU_GROUPS
FFN_DIM = 5632
EPS = 1e-6
LN_EPS = 1e-5

SUBLANES_F32 = 8
LANES = 128
VMEM_LIMIT_BYTES = 60 * 1024 * 1024

PROJ_TM = 1024
PROJ_TN = 2048
PROJ_SUB = 512
RET_BLOCK = 256
OUT_TM = 512
SGU_TM = 512
FFN_TM = 1024
FFN_TF = 512
FFN_HALF = FFN_TF // 2
HALO = SUBLANES_F32

BF16 = jnp.bfloat16
F32 = jnp.float32


def _params(*semantics):
    return pltpu.CompilerParams(dimension_semantics=semantics,
                                vmem_limit_bytes=VMEM_LIMIT_BYTES)


def _rms_rows(x, gain):
    ms = jnp.mean(x * x, axis=-1, keepdims=True)
    return x * lax.rsqrt(ms + EPS) * gain


def _dot(a, b):
    return jnp.dot(a, b, preferred_element_type=F32)


def _silu(a):
    return a * jax.nn.sigmoid(a)


def _ret_proj_kernel(x_ref, gain_ref, w_ref, cos_ref, sin_ref, o_ref, hn_ref,
                     *, n_plain, n_silu, n_query):
    j = pl.program_id(2)

    @pl.when(j == 0)
    def _():
        hn_ref[...] = _rms_rows(x_ref[0], gain_ref[...]).astype(BF16)

    def store_plain(acc, lo):
        o_ref[0, :, lo:lo + PROJ_SUB] = acc.astype(BF16)

    def store_silu(acc, lo):
        o_ref[0, :, lo:lo + PROJ_SUB] = _silu(acc).astype(BF16)

    def store_rope(acc, lo):
        cos = cos_ref[...]
        sin = sin_ref[...]
        half = RET_DK // 2
        scale = jnp.where(j >= n_plain + n_silu + n_query, RET_DK ** -0.5, 1.0).astype(F32)
        for hh in range(PROJ_SUB // RET_DK):
            a = hh * RET_DK
            x1 = acc[:, a:a + half]
            x2 = acc[:, a + half:a + RET_DK]
            o_ref[0, :, lo + a:lo + a + half] = ((x1 * cos - x2 * sin) * scale).astype(BF16)
            o_ref[0, :, lo + a + half:lo + a + RET_DK] = ((x2 * cos + x1 * sin) * scale).astype(BF16)

    def chains(store):
        for s in range(PROJ_TN // PROJ_SUB):
            lo = s * PROJ_SUB
            store(_dot(hn_ref[...], w_ref[:, lo:lo + PROJ_SUB]), lo)

    @pl.when(j < n_plain)
    def _():
        chains(store_plain)

    @pl.when(jnp.logical_and(j >= n_plain, j < n_plain + n_silu))
    def _():
        chains(store_silu)

    @pl.when(j >= n_plain + n_silu)
    def _():
        chains(store_rope)


def _ret_proj(x, gain, w_vgqk, cos, sin):
    B, S, D = x.shape
    N = w_vgqk.shape[1]
    tm, tn = PROJ_TM, PROJ_TN
    half = RET_DK // 2
    kern = functools.partial(_ret_proj_kernel, n_plain=RET_HV // tn, n_silu=RET_HV // tn,
                             n_query=RET_HQ // tn)
    return pl.pallas_call(
        kern,
        grid=(B, S // tm, N // tn),
        in_specs=[
            pl.BlockSpec((1, tm, D), lambda b, i, j: (b, i, 0)),
            pl.BlockSpec((1, D), lambda b, i, j: (0, 0)),
            pl.BlockSpec((D, tn), lambda b, i, j: (0, j)),
            pl.BlockSpec((tm, half), lambda b, i, j: (i, 0)),
            pl.BlockSpec((tm, half), lambda b, i, j: (i, 0)),
        ],
        out_specs=pl.BlockSpec((1, tm, tn), lambda b, i, j: (b, i, j)),
        out_shape=jax.ShapeDtypeStruct((B, S, N), BF16),
        scratch_shapes=[pltpu.VMEM((tm, D), BF16)],
        compiler_params=_params("parallel", "parallel", "arbitrary"),
        name="ret_proj",
    )(x, gain, w_vgqk, cos, sin)


def _retention_kernel(lg_ref, q_ref, k_ref, v_ref, *rest, reverse, finish):
    if finish:
        yb_ref, gate_ref, gain_ref, o_ref = rest[:4]
        state_ref, dec_ref, qd_ref, kd_ref, cd_ref = rest[4:]
    else:
        o_ref = rest[0]
        state_ref, dec_ref, qd_ref, kd_ref, cd_ref = rest[1:]
    C = RET_BLOCK
    t = pl.program_id(1)
    last_dims = (((1,), (1,)), ((), ()))
    first_dims = (((0,), (0,)), ((), ()))

    @pl.when(t == 0)
    def _():
        state_ref[...] = jnp.zeros_like(state_ref)
        ri = lax.broadcasted_iota(jnp.int32, (C, C), 0)
        ci = lax.broadcasted_iota(jnp.int32, (C, C), 1)
        d = (ci - ri) if reverse else (ri - ci)
        mask = (d > 0) if reverse else (d >= 0)
        dist = jnp.maximum(d, 0).astype(F32)
        qrow = lax.broadcasted_iota(jnp.int32, (C, RET_DV), 0).astype(F32)
        krow = lax.broadcasted_iota(jnp.int32, (C, RET_DK), 0).astype(F32)
        for h in range(RET_HEADS):
            lg = lg_ref[h]
            dec_ref[h] = jnp.where(mask, jnp.exp(lg * dist), 0.0)
            if reverse:
                qd_ref[h] = jnp.exp(lg * (C - qrow))
                kd_ref[h] = jnp.exp(lg * krow)
            else:
                qd_ref[h] = jnp.exp(lg * (qrow + 1.0))
                kd_ref[h] = jnp.exp(lg * (C - 1.0 - krow))
            cd_ref[h] = jnp.exp(jnp.full((1, RET_DV), lg * C, F32))

    for h in range(RET_HEADS):
        kcols = slice(h * RET_DK, (h + 1) * RET_DK)
        vcols = slice(h * RET_DV, (h + 1) * RET_DV)
        q = q_ref[0, :, kcols]
        k = k_ref[0, :, kcols]
        v = v_ref[0, :, vcols]
        scores = lax.dot_general(q, k, last_dims, preferred_element_type=F32) * dec_ref[h]
        y = _dot(scores.astype(BF16), v)
        y = y + qd_ref[h] * _dot(q, state_ref[h].astype(BF16))
        ks = (k.astype(F32) * kd_ref[h]).astype(BF16)
        state_ref[h] = (state_ref[h] * cd_ref[h]
                        + lax.dot_general(ks, v, first_dims, preferred_element_type=F32))
        if finish:
            y = y + yb_ref[0, :, vcols].astype(F32)
            mu = jnp.mean(y, axis=-1, keepdims=True)
            yc = y - mu
            var = jnp.mean(yc * yc, axis=-1, keepdims=True)
            yn = yc * lax.rsqrt(var + LN_EPS) * gain_ref[:, vcols]
            o_ref[0, :, vcols] = (gate_ref[0, :, vcols].astype(F32) * yn).astype(BF16)
        else:
            o_ref[0, :, vcols] = y.astype(BF16)


def _retention(vgqk, log_gamma, *, reverse, y_other=None, gn_gain=None):
    B, S, _ = vgqk.shape
    C = RET_BLOCK
    H = RET_HEADS
    nC = S // C
    finish = y_other is not None
    q_blk = 2 * RET_HV // RET_HQ
    k_blk = q_blk + 1
    if reverse:
        cidx = lambda t: nC - 1 - t
    else:
        cidx = lambda t: t
    in_specs = [
        pl.BlockSpec(memory_space=pltpu.SMEM),
        pl.BlockSpec((1, C, RET_HQ), lambda b, t: (b, cidx(t), q_blk)),
        pl.BlockSpec((1, C, RET_HQ), lambda b, t: (b, cidx(t), k_blk)),
        pl.BlockSpec((1, C, RET_HV), lambda b, t: (b, cidx(t), 0)),
    ]
    args = [log_gamma, vgqk, vgqk, vgqk]
    if finish:
        in_specs += [
            pl.BlockSpec((1, C, RET_HV), lambda b, t: (b, cidx(t), 0)),
            pl.BlockSpec((1, C, RET_HV), lambda b, t: (b, cidx(t), 1)),
            pl.BlockSpec((1, RET_HV), lambda b, t: (0, 0)),
        ]
        args += [y_other, vgqk, gn_gain]
    kern = functools.partial(_retention_kernel, reverse=reverse, finish=finish)
    return pl.pallas_call(
        kern,
        grid=(B, nC),
        in_specs=in_specs,
        out_specs=pl.BlockSpec((1, C, RET_HV), lambda b, t: (b, cidx(t), 0)),
        out_shape=jax.ShapeDtypeStruct((B, S, RET_HV), BF16),
        scratch_shapes=[
            pltpu.VMEM((H, RET_DK, RET_DV), F32),
            pltpu.VMEM((H, C, C), F32),
            pltpu.VMEM((H, C, RET_DV), F32),
            pltpu.VMEM((H, C, RET_DK), F32),
            pltpu.VMEM((H, 1, RET_DV), F32),
        ],
        compiler_params=_params("parallel", "arbitrary"),
        name="retention_rev" if reverse else "retention_fwd",
    )(*args)


def _out_proj_kernel(a_ref, w_ref, x_ref, o_ref):
    N = w_ref.shape[1]
    for s in range(N // PROJ_SUB):
        cols = slice(s * PROJ_SUB, (s + 1) * PROJ_SUB)
        o_ref[0, :, cols] = x_ref[0, :, cols] + _dot(a_ref[0], w_ref[:, cols])


def _out_proj(a, w, x):
    B, S, K = a.shape
    N = w.shape[1]
    tm = OUT_TM
    return pl.pallas_call(
        _out_proj_kernel,
        grid=(B, S // tm),
        in_specs=[
            pl.BlockSpec((1, tm, K), lambda b, i: (b, i, 0)),
            pl.BlockSpec((K, N), lambda b, i: (0, 0), pipeline_mode=pl.Buffered(1)),
            pl.BlockSpec((1, tm, N), lambda b, i: (b, i, 0)),
        ],
        out_specs=pl.BlockSpec((1, tm, N), lambda b, i: (b, i, 0)),
        out_shape=jax.ShapeDtypeStruct((B, S, N), F32),
        compiler_params=_params("parallel", "parallel"),
        name="out_proj",
    )(a, w, x)


def _sgu_in_kernel(x_ref, gain_ref, w_ref, b_ref, z_ref, mean_ref, rstd_ref,
                   hn_ref, s1_ref, s2_ref, *, n_u, n_tiles):
    j = pl.program_id(2)

    @pl.when(j == 0)
    def _():
        hn_ref[...] = _rms_rows(x_ref[0], gain_ref[...]).astype(BF16)
        s1_ref[...] = jnp.zeros_like(s1_ref)
        s2_ref[...] = jnp.zeros_like(s2_ref)

    def chains(with_stats):
        for s in range(PROJ_TN // PROJ_SUB):
            cols = slice(s * PROJ_SUB, (s + 1) * PROJ_SUB)
            acc = _dot(hn_ref[...], w_ref[:, cols]) + b_ref[:, cols]
            z = 0.5 * acc * (1.0 + lax.erf(acc * np.float32(np.sqrt(0.5))))
            z_ref[0, :, cols] = z.astype(BF16)
            if with_stats:
                zz = z * z
                p1 = z[:, 0:LANES]
                p2 = zz[:, 0:LANES]
                for c in range(1, PROJ_SUB // LANES):
                    p1 = p1 + z[:, c * LANES:(c + 1) * LANES]
                    p2 = p2 + zz[:, c * LANES:(c + 1) * LANES]
                s1_ref[...] += p1
                s2_ref[...] += p2

    @pl.when(j < n_u)
    def _():
        chains(False)

    @pl.when(j >= n_u)
    def _():
        chains(True)

    @pl.when(j == n_tiles - 1)
    def _():
        mean = jnp.sum(s1_ref[...], axis=-1, keepdims=True) * (1.0 / SGU_E)
        var = jnp.sum(s2_ref[...], axis=-1, keepdims=True) * (1.0 / SGU_E) - mean * mean
        mean_ref[0] = mean
        rstd_ref[0] = lax.rsqrt(var + LN_EPS)


def _sgu_in(x, gain, w, b):
    B, S, D = x.shape
    N = w.shape[1]
    tm, tn = PROJ_TM, PROJ_TN
    n_tiles = N // tn
    kern = functools.partial(_sgu_in_kernel, n_u=SGU_E // tn, n_tiles=n_tiles)
    return pl.pallas_call(
        kern,
        grid=(B, S // tm, n_tiles),
        in_specs=[
            pl.BlockSpec((1, tm, D), lambda b_, i, j: (b_, i, 0)),
            pl.BlockSpec((1, D), lambda b_, i, j: (0, 0)),
            pl.BlockSpec((D, tn), lambda b_, i, j: (0, j)),
            pl.BlockSpec((1, tn), lambda b_, i, j: (0, j)),
        ],
        out_specs=[
            pl.BlockSpec((1, tm, tn), lambda b_, i, j: (b_, i, j)),
            pl.BlockSpec((1, tm, 1), lambda b_, i, j: (b_, i, 0)),
            pl.BlockSpec((1, tm, 1), lambda b_, i, j: (b_, i, 0)),
        ],
        out_shape=[
            jax.ShapeDtypeStruct((B, S, N), BF16),
            jax.ShapeDtypeStruct((B, S, 1), F32),
            jax.ShapeDtypeStruct((B, S, 1), F32),
        ],
        scratch_shapes=[
            pltpu.VMEM((tm, D), BF16),
            pltpu.VMEM((tm, LANES), F32),
            pltpu.VMEM((tm, LANES), F32),
        ],
        compiler_params=_params("parallel", "parallel", "arbitrary"),
        name="sgu_in",
    )(x, gain, w, b)


def _sgu_out_kernel(u_ref, v_ref, mean_ref, rstd_ref, lng_ref, lnb_ref, ws_ref,
                    bs_ref, wo_prev_ref, wo_ref, wo_tail_ref, x_ref, o_ref, acc_ref, ga_ref,
                    gb_ref,
                    *, n_pairs):
    s = pl.program_id(2)
    gw = SGU_GW

    @pl.when(s == 0)
    def _():
        acc_ref[...] = jnp.zeros_like(acc_ref)
        gb_ref[...] = jnp.zeros_like(gb_ref)

    def gate_group(k, out_ref):
        g = 2 * s + k
        cols = slice(k * gw, (k + 1) * gw)
        vn = ((v_ref[0, :, cols].astype(F32) - mean_ref[0]) * rstd_ref[0] * lng_ref[g]
              + lnb_ref[g]).astype(BF16)
        ws = ws_ref[g]
        bs = bs_ref[g]
        for c in range(SGU_TM // SGU_CHUNK):
            rows = slice(c * SGU_CHUNK, (c + 1) * SGU_CHUNK)
            mixed = _dot(ws, vn[rows]) + bs
            out_ref[rows, :] = (u_ref[0, rows, cols].astype(F32) * mixed).astype(BF16)

    acc_ref[...] += _dot(gb_ref[...], wo_prev_ref[...])
    gate_group(0, ga_ref)
    acc_ref[...] += _dot(ga_ref[...], wo_ref[...])
    gate_group(1, gb_ref)

    @pl.when(s == n_pairs - 1)
    def _():
        o_ref[0] = x_ref[0] + acc_ref[...] + _dot(gb_ref[...], wo_tail_ref[...])


def _sgu_out(z, mean, rstd, ln_g, ln_b, w_s, b_s, w_out, x):
    B, S, D = x.shape
    tm = SGU_TM
    G = SGU_GROUPS
    gw = SGU_GW
    n_pairs = G // 2
    kern = functools.partial(_sgu_out_kernel, n_pairs=n_pairs)
    whole = lambda a: pl.BlockSpec(a.shape, lambda b, i, s: (0,) * a.ndim)
    return pl.pallas_call(
        kern,
        grid=(B, S // tm, n_pairs),
        in_specs=[
            pl.BlockSpec((1, tm, 2 * gw), lambda b, i, s: (b, i, s)),
            pl.BlockSpec((1, tm, 2 * gw), lambda b, i, s: (b, i, n_pairs + s)),
            pl.BlockSpec((1, tm, 1), lambda b, i, s: (b, i, 0)),
            pl.BlockSpec((1, tm, 1), lambda b, i, s: (b, i, 0)),
            whole(ln_g),
            whole(ln_b),
            whole(w_s),
            whole(b_s),
            pl.BlockSpec((gw, D), lambda b, i, s: (jnp.maximum(2 * s - 1, 0), 0)),
            pl.BlockSpec((gw, D), lambda b, i, s: (2 * s, 0)),
            pl.BlockSpec((gw, D), lambda b, i, s: (G - 1, 0)),
            pl.BlockSpec((1, tm, D), lambda b, i, s: (b, i, 0)),
        ],
        out_specs=pl.BlockSpec((1, tm, D), lambda b, i, s: (b, i, 0)),
        out_shape=jax.ShapeDtypeStruct((B, S, D), F32),
        scratch_shapes=[
            pltpu.VMEM((tm, D), F32),
            pltpu.VMEM((tm, gw), BF16),
            pltpu.VMEM((tm, gw), BF16),
        ],
        compiler_params=_params("parallel", "parallel", "arbitrary"),
        name="sgu_out",
    )(z, z, mean, rstd, ln_g, ln_b, w_s, b_s, w_out, w_out, w_out, x)


def _stage(a, scr_ref, first):
    tm = FFN_TM
    for c in range(FFN_HALF // LANES):
        blk = a[:, c * LANES:(c + 1) * LANES]
        scr_ref[first + c, 0:HALO, :] = blk[tm + HALO:tm + 2 * HALO]
        scr_ref[first + c, HALO:tm + 2 * HALO, :] = blk[:tm + HALO]


def _conv3(scr_ref, slab, cw, cb):
    tm = FFN_TM
    prev = scr_ref[slab, HALO - 1:HALO - 1 + tm, :]
    cur = scr_ref[slab, HALO:HALO + tm, :]
    nxt = scr_ref[slab, HALO + 1:HALO + 1 + tm, :]
    return prev * cw[0:1, :] + cur * cw[1:2, :] + nxt * cw[2:3, :] + cb


def _conv_ffn_kernel(x_hbm, xprev_ref, xnext_ref, xprev_n_ref, xnext_n_ref, gain_ref,
                     wg_ref, wu_ref, cw_ref, wo_ref, fin_ref, o_ref, xs_ref, xs_sem,
                     hn_ref, acta_ref, actb_ref, sa_ref, sb_ref,
                     *, n_row_tiles, n_f, final_norm):
    b = pl.program_id(0)
    i = pl.program_id(1)
    j = pl.program_id(2)
    tm = FFN_TM
    nblk = FFN_HALF // LANES
    up0 = FFN_TF
    n_tiles = pl.num_programs(0) * n_row_tiles
    tile = b * n_row_tiles + i
    slot = tile % 2
    nxt = tile + 1
    nxt_b = nxt // n_row_tiles
    nxt_i = nxt % n_row_tiles

    def x_copy(bb, ii):
        return pltpu.make_async_copy(x_hbm.at[bb, pl.ds(ii * tm, tm), :], xs_ref, xs_sem)

    def write_hn(dst, prev_rows, next_rows, ii):
        gain = gain_ref[...]
        hn_ref[dst, 0:tm, :] = _rms_rows(xs_ref[...], gain).astype(BF16)
        has_next = (ii < n_row_tiles - 1).astype(F32)
        has_prev = (ii > 0).astype(F32)
        halo = jnp.concatenate([_rms_rows(next_rows, gain) * has_next,
                                _rms_rows(prev_rows, gain) * has_prev], axis=0)
        hn_ref[dst, tm:tm + 2 * HALO, :] = halo.astype(BF16)

    @pl.when(j == 0)
    def _():
        @pl.when(tile == 0)
        def _():
            x_copy(b, i).start()
            x_copy(b, i).wait()
            write_hn(slot, xprev_ref[0], xnext_ref[0], i)

        o_ref[0] = xs_ref[...]

    @pl.when(jnp.logical_and(j == 1, nxt < n_tiles))
    def _():
        x_copy(nxt_b, nxt_i).start()

    def in_proj(lo, scr_ref):
        hn = hn_ref[slot]
        _stage(_dot(hn, wg_ref[:, lo:lo + FFN_HALF]), scr_ref, 0)
        _stage(_dot(hn, wu_ref[:, lo:lo + FFN_HALF]), scr_ref, nblk)

    def conv_act(scr_ref, lo, act_ref):
        cw = cw_ref[j]
        for c in range(nblk):
            g0 = lo + c * LANES
            u0 = up0 + g0
            gate = _conv3(scr_ref, c, cw[0:3, g0:g0 + LANES], cw[3:4, g0:g0 + LANES])
            up = _conv3(scr_ref, nblk + c, cw[0:3, u0:u0 + LANES], cw[3:4, u0:u0 + LANES])
            act_ref[:, c * LANES:(c + 1) * LANES] = (_silu(gate) * up).astype(BF16)

    def step(prepare_next):
        in_proj(0, sa_ref)
        if prepare_next:
            write_hn(1 - slot, xprev_n_ref[0], xnext_n_ref[0], nxt_i)
        in_proj(FFN_HALF, sb_ref)
        conv_act(sa_ref, 0, acta_ref)
        o_ref[0] += _dot(acta_ref[...], wo_ref[0:FFN_HALF, :])
        conv_act(sb_ref, FFN_HALF, actb_ref)
        o_ref[0] += _dot(actb_ref[...], wo_ref[FFN_HALF:FFN_TF, :])

    @pl.when(j < n_f - 1)
    def _():
        step(False)

    @pl.when(j == n_f - 1)
    def _():
        @pl.when(nxt < n_tiles)
        def _():
            x_copy(nxt_b, nxt_i).wait()

        step(True)
        if final_norm:
            o_ref[0] = _rms_rows(o_ref[0], fin_ref[...])


def _conv_ffn(x, gain, w_in, conv_tiles, w_out, fin_gain, *, final_norm):
    B, S, D = x.shape
    tm, tf = FFN_TM, FFN_TF
    n_f = FFN_DIM // tf
    n_row_tiles = S // tm
    hb = tm // HALO
    last_hb = S // HALO - 1
    kern = functools.partial(_conv_ffn_kernel, n_row_tiles=n_row_tiles, n_f=n_f,
                             final_norm=final_norm)

    def halo_specs(tile_of):
        def before(b, i, j):
            bb, ii = tile_of(b, i)
            return (bb, jnp.maximum(ii * hb - 1, 0), 0)

        def after(b, i, j):
            bb, ii = tile_of(b, i)
            return (bb, jnp.minimum((ii + 1) * hb, last_hb), 0)

        return [pl.BlockSpec((1, HALO, D), before), pl.BlockSpec((1, HALO, D), after)]

    def next_tile(b, i):
        nxt = b * n_row_tiles + i + 1
        return jnp.minimum(nxt // n_row_tiles, B - 1), nxt % n_row_tiles

    return pl.pallas_call(
        kern,
        grid=(B, n_row_tiles, n_f),
        in_specs=[pl.BlockSpec(memory_space=pl.ANY)]
        + halo_specs(lambda b, i: (b, i))
        + halo_specs(next_tile)
        + [
            pl.BlockSpec((1, D), lambda b, i, j: (0, 0)),
            pl.BlockSpec((D, tf), lambda b, i, j: (0, j)),
            pl.BlockSpec((D, tf), lambda b, i, j: (0, n_f + j)),
            pl.BlockSpec(conv_tiles.shape, lambda b, i, j: (0, 0, 0)),
            pl.BlockSpec((tf, D), lambda b, i, j: (j, 0)),
            pl.BlockSpec((1, D), lambda b, i, j: (0, 0)),
        ],
        out_specs=pl.BlockSpec((1, tm, D), lambda b, i, j: (b, i, 0)),
        out_shape=jax.ShapeDtypeStruct((B, S, D), F32),
        scratch_shapes=[
            pltpu.VMEM((tm, D), F32),
            pltpu.SemaphoreType.DMA(()),
            pltpu.VMEM((2, tm + 2 * HALO, D), BF16),
            pltpu.VMEM((tm, FFN_HALF), BF16),
            pltpu.VMEM((tm, FFN_HALF), BF16),
            pltpu.VMEM((2 * FFN_HALF // LANES, tm + 2 * HALO, LANES), F32),
            pltpu.VMEM((2 * FFN_HALF // LANES, tm + 2 * HALO, LANES), F32),
        ],
        compiler_params=_params("arbitrary", "arbitrary", "arbitrary"),
        name="conv_ffn",
    )(x, x, x, x, x, gain, w_in, w_in, conv_tiles, w_out, fin_gain)


def _conv_tiles(conv_w, conv_b):
    n_f = FFN_DIM // FFN_TF
    a = jnp.concatenate([conv_w, conv_b.reshape(1, -1)], axis=0)
    a = a.reshape(4, 2, n_f, FFN_TF)
    return a.transpose(2, 0, 1, 3).reshape(n_f, 4, 2 * FFN_TF)


def _rope_tables(S):
    half = RET_DK // 2
    inv = ROPE_BASE ** (-jnp.arange(half, dtype=F32) / half)
    ang = jnp.arange(S, dtype=F32)[:, None] * inv[None, :]
    return jnp.cos(ang), jnp.sin(ang)


def _row(v):
    return v.reshape(1, -1)


def _prepare(ln_mix, ln_ffn, ret_w_in, ret_log_decay_fwd, ret_log_decay_bwd,
             ret_gn_gain, ret_w_out, sgu_w_in, sgu_b_in, sgu_ln_g, sgu_ln_b,
             sgu_w_s, sgu_b_s, sgu_w_out, ffn_w_in, ffn_conv_w, ffn_conv_b,
             ffn_w_out, ln_final):
    depth = ln_mix.shape[0]
    hq = RET_HQ
    layers = []
    for i in range(depth):
        j = i // 2
        p = {"ln_mix": _row(ln_mix[i]), "ln_ffn": _row(ln_ffn[i])}
        if i % 2 == 0:
            w = ret_w_in[j]
            p["w_vgqk"] = jnp.concatenate([w[:, 2 * hq:], w[:, :2 * hq]], axis=1).astype(BF16)
            p["lg_fwd"] = ret_log_decay_fwd[j]
            p["lg_bwd"] = ret_log_decay_bwd[j]
            p["gn_gain"] = _row(ret_gn_gain[j])
            p["w_out"] = ret_w_out[j].astype(BF16)
        else:
            p["w_in"] = sgu_w_in[j].astype(BF16)
            p["b_in"] = _row(sgu_b_in[j])
            p["ln_g"] = sgu_ln_g[j].reshape(SGU_GROUPS, 1, SGU_GW)
            p["ln_b"] = sgu_ln_b[j].reshape(SGU_GROUPS, 1, SGU_GW)
            p["w_s"] = sgu_w_s[j].astype(BF16)
            p["b_s"] = sgu_b_s[j][:, :, None]
            p["w_out"] = sgu_w_out[j].astype(BF16)
        p["ffn_w_in"] = ffn_w_in[i].astype(BF16)
        p["ffn_conv"] = _conv_tiles(ffn_conv_w[i], ffn_conv_b[i])
        p["ffn_w_out"] = ffn_w_out[i].astype(BF16)
        layers.append(p)
    return layers, _row(ln_final)


def _trunk(x, layers, fin_gain, cos, sin):
    for i, p in enumerate(layers):
        if i % 2 == 0:
            vgqk = _ret_proj(x, p["ln_mix"], p["w_vgqk"], cos, sin)
            y_rev = _retention(vgqk, p["lg_bwd"], reverse=True)
            a = _retention(vgqk, p["lg_fwd"], reverse=False, y_other=y_rev,
                           gn_gain=p["gn_gain"])
            x = _out_proj(a, p["w_out"], x)
        else:
            z, mean, rstd = _sgu_in(x, p["ln_mix"], p["w_in"], p["b_in"])
            x = _sgu_out(z, mean, rstd, p["ln_g"], p["ln_b"], p["w_s"], p["b_s"],
                         p["w_out"], x)
        x = _conv_ffn(x, p["ln_ffn"], p["ffn_w_in"], p["ffn_conv"], p["ffn_w_out"],
                      fin_gain, final_norm=(i == len(layers) - 1))
    return x


def kernel(x_prompt, x_sample, ln_mix, ln_ffn, ret_w_in, ret_log_decay_fwd, ret_log_decay_bwd, ret_gn_gain, ret_w_out, sgu_w_in, sgu_b_in, sgu_ln_g, sgu_ln_b, sgu_w_s, sgu_b_s, sgu_w_out, ffn_w_in, ffn_conv_w, ffn_conv_b, ffn_w_out, ln_final):
    layers, fin_gain = _prepare(
        ln_mix, ln_ffn, ret_w_in, ret_log_decay_fwd, ret_log_decay_bwd, ret_gn_gain,
        ret_w_out, sgu_w_in, sgu_b_in, sgu_ln_g, sgu_ln_b, sgu_w_s, sgu_b_s,
        sgu_w_out, ffn_w_in, ffn_conv_w, ffn_conv_b, ffn_w_out, ln_final)
    cos, sin = _rope_tables(max(x_prompt.shape[1], x_sample.shape[1]))
    return (_trunk(x_prompt, layers, fin_gain, cos, sin),
            _trunk(x_sample, layers, fin_gain, cos, sin))
```

```python
import functools

import jax
import jax.numpy as jnp
import numpy as np
from jax import lax
from jax.experimental import pallas as pl
from jax.experimental.pallas import tpu as pltpu

D_MODEL = 2048
RET_HEADS = 8
RET_DK = D_MODEL // RET_HEADS
RET_DV = 2 * D_MODEL // RET_HEADS
RET_HQ = RET_HEADS * RET_DK
RET_HV = RET_HEADS * RET_DV
ROPE_BASE = 10000.0
SGU_E = 3 * D_MODEL
SGU_GROUPS = 8
SGU_CHUNK = 128
SGU_GW = SGU_E // SGU_GROUPS
FFN_DIM = 5632
EPS = 1e-6
LN_EPS = 1e-5

SUBLANES_F32 = 8
LANES = 128
VMEM_LIMIT_BYTES = 60 * 1024 * 1024

PROJ_TM = 1024
PROJ_TN = 2048
PROJ_SUB = 512
RET_BLOCK = 256
OUT_TM = 512
SGU_TM = 512
FFN_TM = 1024
FFN_TF = 512
FFN_HALF = FFN_TF // 2
HALO = SUBLANES_F32

BF16 = jnp.bfloat16
F32 = jnp.float32


def _params(*semantics):
    return pltpu.CompilerParams(dimension_semantics=semantics,
                                vmem_limit_bytes=VMEM_LIMIT_BYTES)


def _rms_rows(x, gain):
    ms = jnp.mean(x * x, axis=-1, keepdims=True)
    return x * lax.rsqrt(ms + EPS) * gain


def _dot(a, b):
    return jnp.dot(a, b, preferred_element_type=F32)


def _silu(a):
    return a * jax.nn.sigmoid(a)


def _ret_proj_kernel(x_ref, gain_ref, w_ref, cos_ref, sin_ref, o_ref, hn_ref,
                     *, n_plain, n_silu, n_query):
    j = pl.program_id(2)

    @pl.when(j == 0)
    def _():
        hn_ref[...] = _rms_rows(x_ref[0], gain_ref[...]).astype(BF16)

    def store_plain(acc, lo):
        o_ref[0, :, lo:lo + PROJ_SUB] = acc.astype(BF16)

    def store_silu(acc, lo):
        o_ref[0, :, lo:lo + PROJ_SUB] = _silu(acc).astype(BF16)

    def store_rope(acc, lo):
        cos = cos_ref[...]
        sin = sin_ref[...]
        half = RET_DK // 2
        scale = jnp.where(j >= n_plain + n_silu + n_query, RET_DK ** -0.5, 1.0).astype(F32)
        for hh in range(PROJ_SUB // RET_DK):
            a = hh * RET_DK
            x1 = acc[:, a:a + half]
            x2 = acc[:, a + half:a + RET_DK]
            o_ref[0, :, lo + a:lo + a + half] = ((x1 * cos - x2 * sin) * scale).astype(BF16)
            o_ref[0, :, lo + a + half:lo + a + RET_DK] = ((x2 * cos + x1 * sin) * scale).astype(BF16)

    def chains(store):
        for s in range(PROJ_TN // PROJ_SUB):
            lo = s * PROJ_SUB
            store(_dot(hn_ref[...], w_ref[:, lo:lo + PROJ_SUB]), lo)

    @pl.when(j < n_plain)
    def _():
        chains(store_plain)

    @pl.when(jnp.logical_and(j >= n_plain, j < n_plain + n_silu))
    def _():
        chains(store_silu)

    @pl.when(j >= n_plain + n_silu)
    def _():
        chains(store_rope)


def _ret_proj(x, gain, w_vgqk, cos, sin):
    B, S, D = x.shape
    N = w_vgqk.shape[1]
    tm, tn = PROJ_TM, PROJ_TN
    half = RET_DK // 2
    kern = functools.partial(_ret_proj_kernel, n_plain=RET_HV // tn, n_silu=RET_HV // tn,
                             n_query=RET_HQ // tn)
    return pl.pallas_call(
        kern,
        grid=(B, S // tm, N // tn),
        in_specs=[
            pl.BlockSpec((1, tm, D), lambda b, i, j: (b, i, 0)),
            pl.BlockSpec((1, D), lambda b, i, j: (0, 0)),
            pl.BlockSpec((D, tn), lambda b, i, j: (0, j)),
            pl.BlockSpec((tm, half), lambda b, i, j: (i, 0)),
            pl.BlockSpec((tm, half), lambda b, i, j: (i, 0)),
        ],
        out_specs=pl.BlockSpec((1, tm, tn), lambda b, i, j: (b, i, j)),
        out_shape=jax.ShapeDtypeStruct((B, S, N), BF16),
        scratch_shapes=[pltpu.VMEM((tm, D), BF16)],
        compiler_params=_params("parallel", "parallel", "arbitrary"),
        name="ret_proj",
    )(x, gain, w_vgqk, cos, sin)


_A_BT = (((1,), (1,)), ((), ()))
_AT_B = (((0,), (0,)), ((), ()))


def _ret_state_kernel(lg_ref, k_ref, v_ref, s_ref, state_ref, kd_ref, cd_ref):
    C = RET_BLOCK
    t = pl.program_id(1)

    @pl.when(t == 0)
    def _():
        state_ref[...] = jnp.zeros_like(state_ref)
        krow = lax.broadcasted_iota(jnp.int32, (C, RET_DK), 0).astype(F32)
        for h in range(RET_HEADS):
            lg = lg_ref[h]
            kd_ref[h] = jnp.exp(lg * krow)
            cd_ref[h] = jnp.exp(jnp.full((1, RET_DV), lg * C, F32))

    for h in range(RET_HEADS):
        kcols = slice(h * RET_DK, (h + 1) * RET_DK)
        vcols = slice(h * RET_DV, (h + 1) * RET_DV)
        s_ref[0, 0, kcols, :] = state_ref[h].astype(BF16)
        ks = (k_ref[0, :, kcols].astype(F32) * kd_ref[h]).astype(BF16)
        state_ref[h] = (state_ref[h] * cd_ref[h]
                        + lax.dot_general(ks, v_ref[0, :, vcols], _AT_B,
                                          preferred_element_type=F32))


def _ret_state(vgqk, log_gamma):
    B, S, _ = vgqk.shape
    C = RET_BLOCK
    H = RET_HEADS
    nC = S // C
    k_blk = 2 * RET_HV // RET_HQ + 1
    return pl.pallas_call(
        _ret_state_kernel,
        grid=(B, nC),
        in_specs=[
            pl.BlockSpec(memory_space=pltpu.SMEM),
            pl.BlockSpec((1, C, RET_HQ), lambda b, t: (b, nC - 1 - t, k_blk)),
            pl.BlockSpec((1, C, RET_HV), lambda b, t: (b, nC - 1 - t, 0)),
        ],
        out_specs=pl.BlockSpec((1, 1, RET_HQ, RET_DV), lambda b, t: (b, nC - 1 - t, 0, 0)),
        out_shape=jax.ShapeDtypeStruct((B, nC, RET_HQ, RET_DV), BF16),
        scratch_shapes=[
            pltpu.VMEM((H, RET_DK, RET_DV), F32),
            pltpu.VMEM((H, C, RET_DK), F32),
            pltpu.VMEM((H, 1, RET_DV), F32),
        ],
        compiler_params=_params("parallel", "arbitrary"),
        name="retention_state",
    )(log_gamma, vgqk, vgqk)


def _retention_kernel(lgf_ref, lgb_ref, q_ref, k_ref, v_ref, srev_ref, gate_ref, gain_ref,
                      o_ref, state_ref, dec_ref, qdf_ref, qdb_ref, kd_ref, cd_ref):
    C = RET_BLOCK
    t = pl.program_id(1)

    @pl.when(t == 0)
    def _():
        state_ref[...] = jnp.zeros_like(state_ref)
        ri = lax.broadcasted_iota(jnp.int32, (C, C), 0)
        ci = lax.broadcasted_iota(jnp.int32, (C, C), 1)
        behind = jnp.maximum(ri - ci, 0).astype(F32)
        ahead = jnp.maximum(ci - ri, 0).astype(F32)
        qrow = lax.broadcasted_iota(jnp.int32, (C, RET_DV), 0).astype(F32)
        krow = lax.broadcasted_iota(jnp.int32, (C, RET_DK), 0).astype(F32)
        for h in range(RET_HEADS):
            lgf = lgf_ref[h]
            lgb = lgb_ref[h]
            dec_ref[h] = jnp.where(ri >= ci, jnp.exp(lgf * behind), jnp.exp(lgb * ahead))
            qdf_ref[h] = jnp.exp(lgf * (qrow + 1.0))
            qdb_ref[h] = jnp.exp(lgb * (C - qrow))
            kd_ref[h] = jnp.exp(lgf * (C - 1.0 - krow))
            cd_ref[h] = jnp.exp(jnp.full((1, RET_DV), lgf * C, F32))

    for h in range(RET_HEADS):
        kcols = slice(h * RET_DK, (h + 1) * RET_DK)
        vcols = slice(h * RET_DV, (h + 1) * RET_DV)
        q = q_ref[0, :, kcols]
        k = k_ref[0, :, kcols]
        v = v_ref[0, :, vcols]
        scores = lax.dot_general(q, k, _A_BT, preferred_element_type=F32) * dec_ref[h]
        y = _dot(scores.astype(BF16), v)
        y = y + qdf_ref[h] * _dot(q, state_ref[h].astype(BF16))
        y = y + qdb_ref[h] * _dot(q, srev_ref[0, 0, kcols, :])
        ks = (k.astype(F32) * kd_ref[h]).astype(BF16)
        state_ref[h] = (state_ref[h] * cd_ref[h]
                        + lax.dot_general(ks, v, _AT_B, preferred_element_type=F32))
        mu = jnp.mean(y, axis=-1, keepdims=True)
        yc = y - mu
        var = jnp.mean(yc * yc, axis=-1, keepdims=True)
        yn = yc * lax.rsqrt(var + LN_EPS) * gain_ref[:, vcols]
        o_ref[0, :, vcols] = (gate_ref[0, :, vcols].astype(F32) * yn).astype(BF16)


def _retention(vgqk, s_rev, lg_fwd, lg_bwd, gn_gain):
    B, S, _ = vgqk.shape
    C = RET_BLOCK
    H = RET_HEADS
    q_blk = 2 * RET_HV // RET_HQ
    k_blk = q_blk + 1
    smem = pl.BlockSpec(memory_space=pltpu.SMEM)
    return pl.pallas_call(
        _retention_kernel,
        grid=(B, S // C),
        in_specs=[
            smem,
            smem,
            pl.BlockSpec((1, C, RET_HQ), lambda b, t: (b, t, q_blk)),
            pl.BlockSpec((1, C, RET_HQ), lambda b, t: (b, t, k_blk)),
            pl.BlockSpec((1, C, RET_HV), lambda b, t: (b, t, 0)),
            pl.BlockSpec((1, 1, RET_HQ, RET_DV), lambda b, t: (b, t, 0, 0)),
            pl.BlockSpec((1, C, RET_HV), lambda b, t: (b, t, 1)),
            pl.BlockSpec((1, RET_HV), lambda b, t: (0, 0)),
        ],
        out_specs=pl.BlockSpec((1, C, RET_HV), lambda b, t: (b, t, 0)),
        out_shape=jax.ShapeDtypeStruct((B, S, RET_HV), BF16),
        scratch_shapes=[
            pltpu.VMEM((H, RET_DK, RET_DV), F32),
            pltpu.VMEM((H, C, C), F32),
            pltpu.VMEM((H, C, RET_DV), F32),
            pltpu.VMEM((H, C, RET_DV), F32),
            pltpu.VMEM((H, C, RET_DK), F32),
            pltpu.VMEM((H, 1, RET_DV), F32),
        ],
        compiler_params=_params("parallel", "arbitrary"),
        name="retention",
    )(lg_fwd, lg_bwd, vgqk, vgqk, vgqk, s_rev, vgqk, gn_gain)


def _out_proj_kernel(a_ref, w_ref, x_ref, o_ref):
    N = w_ref.shape[1]
    for s in range(N // PROJ_SUB):
        cols = slice(s * PROJ_SUB, (s + 1) * PROJ_SUB)
        o_ref[0, :, cols] = x_ref[0, :, cols] + _dot(a_ref[0], w_ref[:, cols])


def _out_proj(a, w, x):
    B, S, K = a.shape
    N = w.shape[1]
    tm = OUT_TM
    return pl.pallas_call(
        _out_proj_kernel,
        grid=(B, S // tm),
        in_specs=[
            pl.BlockSpec((1, tm, K), lambda b, i: (b, i, 0)),
            pl.BlockSpec((K, N), lambda b, i: (0, 0), pipeline_mode=pl.Buffered(1)),
            pl.BlockSpec((1, tm, N), lambda b, i: (b, i, 0)),
        ],
        out_specs=pl.BlockSpec((1, tm, N), lambda b, i: (b, i, 0)),
        out_shape=jax.ShapeDtypeStruct((B, S, N), F32),
        compiler_params=_params("parallel", "parallel"),
        name="out_proj",
    )(a, w, x)


def _sgu_in_kernel(x_ref, gain_ref, w_ref, b_ref, z_ref, mean_ref, rstd_ref,
                   hn_ref, s1_ref, s2_ref, *, n_u, n_tiles):
    j = pl.program_id(2)

    @pl.when(j == 0)
    def _():
        hn_ref[...] = _rms_rows(x_ref[0], gain_ref[...]).astype(BF16)
        s1_ref[...] = jnp.zeros_like(s1_ref)
        s2_ref[...] = jnp.zeros_like(s2_ref)

    def chains(with_stats):
        for s in range(PROJ_TN // PROJ_SUB):
            cols = slice(s * PROJ_SUB, (s + 1) * PROJ_SUB)
            acc = _dot(hn_ref[...], w_ref[:, cols]) + b_ref[:, cols]
            z = 0.5 * acc * (1.0 + lax.erf(acc * np.float32(np.sqrt(0.5))))
            z_ref[0, :, cols] = z.astype(BF16)
            if with_stats:
                zz = z * z
                p1 = z[:, 0:LANES]
                p2 = zz[:, 0:LANES]
                for c in range(1, PROJ_SUB // LANES):
                    p1 = p1 + z[:, c * LANES:(c + 1) * LANES]
                    p2 = p2 + zz[:, c * LANES:(c + 1) * LANES]
                s1_ref[...] += p1
                s2_ref[...] += p2

    @pl.when(j < n_u)
    def _():
        chains(False)

    @pl.when(j >= n_u)
    def _():
        chains(True)

    @pl.when(j == n_tiles - 1)
    def _():
        mean = jnp.sum(s1_ref[...], axis=-1, keepdims=True) * (1.0 / SGU_E)
        var = jnp.sum(s2_ref[...], axis=-1, keepdims=True) * (1.0 / SGU_E) - mean * mean
        mean_ref[0] = mean
        rstd_ref[0] = lax.rsqrt(var + LN_EPS)


def _sgu_in(x, gain, w, b):
    B, S, D = x.shape
    N = w.shape[1]
    tm, tn = PROJ_TM, PROJ_TN
    n_tiles = N // tn
    kern = functools.partial(_sgu_in_kernel, n_u=SGU_E // tn, n_tiles=n_tiles)
    return pl.pallas_call(
        kern,
        grid=(B, S // tm, n_tiles),
        in_specs=[
            pl.BlockSpec((1, tm, D), lambda b_, i, j: (b_, i, 0)),
            pl.BlockSpec((1, D), lambda b_, i, j: (0, 0)),
            pl.BlockSpec((D, tn), lambda b_, i, j: (0, j)),
            pl.BlockSpec((1, tn), lambda b_, i, j: (0, j)),
        ],
        out_specs=[
            pl.BlockSpec((1, tm, tn), lambda b_, i, j: (b_, i, j)),
            pl.BlockSpec((1, tm, 1), lambda b_, i, j: (b_, i, 0)),
            pl.BlockSpec((1, tm, 1), lambda b_, i, j: (b_, i, 0)),
        ],
        out_shape=[
            jax.ShapeDtypeStruct((B, S, N), BF16),
            jax.ShapeDtypeStruct((B, S, 1), F32),
            jax.ShapeDtypeStruct((B, S, 1), F32),
        ],
        scratch_shapes=[
            pltpu.VMEM((tm, D), BF16),
            pltpu.VMEM((tm, LANES), F32),
            pltpu.VMEM((tm, LANES), F32),
        ],
        compiler_params=_params("parallel", "parallel", "arbitrary"),
        name="sgu_in",
    )(x, gain, w, b)


def _sgu_out_kernel(u_ref, v_ref, mean_ref, rstd_ref, lng_ref, lnb_ref, ws_ref,
                    bs_ref, wo_prev_ref, wo_ref, wo_tail_ref, x_ref, o_ref, acc_ref, ga_ref,
                    gb_ref,
                    *, n_pairs):
    s = pl.program_id(2)
    gw = SGU_GW

    @pl.when(s == 0)
    def _():
        acc_ref[...] = jnp.zeros_like(acc_ref)
        gb_ref[...] = jnp.zeros_like(gb_ref)

    def gate_group(k, out_ref):
        g = 2 * s + k
        cols = slice(k * gw, (k + 1) * gw)
        vn = ((v_ref[0, :, cols].astype(F32) - mean_ref[0]) * rstd_ref[0] * lng_ref[g]
              + lnb_ref[g]).astype(BF16)
        ws = ws_ref[g]
        bs = bs_ref[g]
        for c in range(SGU_TM // SGU_CHUNK):
            rows = slice(c * SGU_CHUNK, (c + 1) * SGU_CHUNK)
            mixed = _dot(ws, vn[rows]) + bs
            out_ref[rows, :] = (u_ref[0, rows, cols].astype(F32) * mixed).astype(BF16)

    acc_ref[...] += _dot(gb_ref[...], wo_prev_ref[...])
    gate_group(0, ga_ref)
    acc_ref[...] += _dot(ga_ref[...], wo_ref[...])
    gate_group(1, gb_ref)

    @pl.when(s == n_pairs - 1)
    def _():
        o_ref[0] = x_ref[0] + acc_ref[...] + _dot(gb_ref[...], wo_tail_ref[...])


def _sgu_out(z, mean, rstd, ln_g, ln_b, w_s, b_s, w_out, x):
    B, S, D = x.shape
    tm = SGU_TM
    G = SGU_GROUPS
    gw = SGU_GW
    n_pairs = G // 2
    kern = functools.partial(_sgu_out_kernel, n_pairs=n_pairs)
    whole = lambda a: pl.BlockSpec(a.shape, lambda b, i, s: (0,) * a.ndim)
    return pl.pallas_call(
        kern,
        grid=(B, S // tm, n_pairs),
        in_specs=[
            pl.BlockSpec((1, tm, 2 * gw), lambda b, i, s: (b, i, s)),
            pl.BlockSpec((1, tm, 2 * gw), lambda b, i, s: (b, i, n_pairs + s)),
            pl.BlockSpec((1, tm, 1), lambda b, i, s: (b, i, 0)),
            pl.BlockSpec((1, tm, 1), lambda b, i, s: (b, i, 0)),
            whole(ln_g),
            whole(ln_b),
            whole(w_s),
            whole(b_s),
            pl.BlockSpec((gw, D), lambda b, i, s: (jnp.maximum(2 * s - 1, 0), 0)),
            pl.BlockSpec((gw, D), lambda b, i, s: (2 * s, 0)),
            pl.BlockSpec((gw, D), lambda b, i, s: (G - 1, 0)),
            pl.BlockSpec((1, tm, D), lambda b, i, s: (b, i, 0)),
        ],
        out_specs=pl.BlockSpec((1, tm, D), lambda b, i, s: (b, i, 0)),
        out_shape=jax.ShapeDtypeStruct((B, S, D), F32),
        scratch_shapes=[
            pltpu.VMEM((tm, D), F32),
            pltpu.VMEM((tm, gw), BF16),
            pltpu.VMEM((tm, gw), BF16),
        ],
        compiler_params=_params("parallel", "parallel", "arbitrary"),
        name="sgu_out",
    )(z, z, mean, rstd, ln_g, ln_b, w_s, b_s, w_out, w_out, w_out, x)


def _stage(a, scr_ref, first):
    tm = FFN_TM
    for c in range(FFN_HALF // LANES):
        blk = a[:, c * LANES:(c + 1) * LANES]
        scr_ref[first + c, 0:HALO, :] = blk[tm + HALO:tm + 2 * HALO]
        scr_ref[first + c, HALO:tm + 2 * HALO, :] = blk[:tm + HALO]


def _conv3(scr_ref, slab, cw, cb):
    tm = FFN_TM
    prev = scr_ref[slab, HALO - 1:HALO - 1 + tm, :]
    cur = scr_ref[slab, HALO:HALO + tm, :]
    nxt = scr_ref[slab, HALO + 1:HALO + 1 + tm, :]
    return prev * cw[0:1, :] + cur * cw[1:2, :] + nxt * cw[2:3, :] + cb


def _conv_ffn_kernel(x_hbm, xprev_ref, xnext_ref, gain_ref, wg_ref, wu_ref, cw_ref,
                     wo_ref, fin_ref, o_ref, xs_ref, xs_sem, hn_ref, acta_ref, actb_ref,
                     sa_ref, sb_ref, *, n_row_tiles, n_f, final_norm):
    b = pl.program_id(0)
    i = pl.program_id(1)
    j = pl.program_id(2)
    tm = FFN_TM
    nblk = FFN_HALF // LANES
    up0 = FFN_TF
    n_tiles = pl.num_programs(0) * n_row_tiles

    def x_copy(bb, ii):
        return pltpu.make_async_copy(x_hbm.at[bb, pl.ds(ii * tm, tm), :], xs_ref, xs_sem)

    @pl.when(j == 0)
    def _():
        @pl.when(jnp.logical_and(b == 0, i == 0))
        def _():
            x_copy(b, i).start()

        x_copy(b, i).wait()
        gain = gain_ref[...]
        hn_ref[0:tm, :] = _rms_rows(xs_ref[...], gain).astype(BF16)
        has_next = (i < n_row_tiles - 1).astype(F32)
        has_prev = (i > 0).astype(F32)
        halo = jnp.concatenate([_rms_rows(xnext_ref[0], gain) * has_next,
                                _rms_rows(xprev_ref[0], gain) * has_prev], axis=0)
        hn_ref[tm:tm + 2 * HALO, :] = halo.astype(BF16)
        o_ref[0] = xs_ref[...]

    @pl.when(j == 1)
    def _():
        nxt = b * n_row_tiles + i + 1

        @pl.when(nxt < n_tiles)
        def _():
            x_copy(nxt // n_row_tiles, nxt % n_row_tiles).start()

    def in_proj(lo, scr_ref):
        hn = hn_ref[...]
        _stage(_dot(hn, wg_ref[:, lo:lo + FFN_HALF]), scr_ref, 0)
        _stage(_dot(hn, wu_ref[:, lo:lo + FFN_HALF]), scr_ref, nblk)

    def conv_act(scr_ref, lo, act_ref):
        cw = cw_ref[j]
        for c in range(nblk):
            g0 = lo + c * LANES
            u0 = up0 + g0
            gate = _conv3(scr_ref, c, cw[0:3, g0:g0 + LANES], cw[3:4, g0:g0 + LANES])
            up = _conv3(scr_ref, nblk + c, cw[0:3, u0:u0 + LANES], cw[3:4, u0:u0 + LANES])
            act_ref[:, c * LANES:(c + 1) * LANES] = (_silu(gate) * up).astype(BF16)

    in_proj(0, sa_ref)
    in_proj(FFN_HALF, sb_ref)
    conv_act(sa_ref, 0, acta_ref)
    o_ref[0] += _dot(acta_ref[...], wo_ref[0:FFN_HALF, :])
    conv_act(sb_ref, FFN_HALF, actb_ref)
    o_ref[0] += _dot(actb_ref[...], wo_ref[FFN_HALF:FFN_TF, :])

    if final_norm:
        @pl.when(j == n_f - 1)
        def _():
            o_ref[0] = _rms_rows(o_ref[0], fin_ref[...])


def _conv_ffn(x, gain, w_in, conv_tiles, w_out, fin_gain, *, final_norm):
    B, S, D = x.shape
    tm, tf = FFN_TM, FFN_TF
    n_f = FFN_DIM // tf
    n_row_tiles = S // tm
    hb = tm // HALO
    last_hb = S // HALO - 1
    kern = functools.partial(_conv_ffn_kernel, n_row_tiles=n_row_tiles, n_f=n_f,
                             final_norm=final_norm)
    return pl.pallas_call(
        kern,
        grid=(B, n_row_tiles, n_f),
        in_specs=[
            pl.BlockSpec(memory_space=pl.ANY),
            pl.BlockSpec((1, HALO, D),
                         lambda b, i, j: (b, jnp.maximum(i * hb - 1, 0), 0)),
            pl.BlockSpec((1, HALO, D),
                         lambda b, i, j: (b, jnp.minimum((i + 1) * hb, last_hb), 0)),
            pl.BlockSpec((1, D), lambda b, i, j: (0, 0)),
            pl.BlockSpec((D, tf), lambda b, i, j: (0, j)),
            pl.BlockSpec((D, tf), lambda b, i, j: (0, n_f + j)),
            pl.BlockSpec(conv_tiles.shape, lambda b, i, j: (0, 0, 0)),
            pl.BlockSpec((tf, D), lambda b, i, j: (j, 0)),
            pl.BlockSpec((1, D), lambda b, i, j: (0, 0)),
        ],
        out_specs=pl.BlockSpec((1, tm, D), lambda b, i, j: (b, i, 0)),
        out_shape=jax.ShapeDtypeStruct((B, S, D), F32),
        scratch_shapes=[
            pltpu.VMEM((tm, D), F32),
            pltpu.SemaphoreType.DMA(()),
            pltpu.VMEM((tm + 2 * HALO, D), BF16),
            pltpu.VMEM((tm, FFN_HALF), BF16),
            pltpu.VMEM((tm, FFN_HALF), BF16),
            pltpu.VMEM((2 * FFN_HALF // LANES, tm + 2 * HALO, LANES), F32),
            pltpu.VMEM((2 * FFN_HALF // LANES, tm + 2 * HALO, LANES), F32),
        ],
        compiler_params=_params("arbitrary", "arbitrary", "arbitrary"),
        name="conv_ffn",
    )(x, x, x, gain, w_in, w_in, conv_tiles, w_out, fin_gain)


def _conv_tiles(conv_w, conv_b):
    n_f = FFN_DIM // FFN_TF
    a = jnp.concatenate([conv_w, conv_b.reshape(1, -1)], axis=0)
    a = a.reshape(4, 2, n_f, FFN_TF)
    return a.transpose(2, 0, 1, 3).reshape(n_f, 4, 2 * FFN_TF)


def _rope_tables(S):
    half = RET_DK // 2
    inv = ROPE_BASE ** (-jnp.arange(half, dtype=F32) / half)
    ang = jnp.arange(S, dtype=F32)[:, None] * inv[None, :]
    return jnp.cos(ang), jnp.sin(ang)


def _row(v):
    return v.reshape(1, -1)


def _prepare(ln_mix, ln_ffn, ret_w_in, ret_log_decay_fwd, ret_log_decay_bwd,
             ret_gn_gain, ret_w_out, sgu_w_in, sgu_b_in, sgu_ln_g, sgu_ln_b,
             sgu_w_s, sgu_b_s, sgu_w_out, ffn_w_in, ffn_conv_w, ffn_conv_b,
             ffn_w_out, ln_final):
    depth = ln_mix.shape[0]
    hq = RET_HQ
    layers = []
    for i in range(depth):
        j = i // 2
        p = {"ln_mix": _row(ln_mix[i]), "ln_ffn": _row(ln_ffn[i])}
        if i % 2 == 0:
            w = ret_w_in[j]
            p["w_vgqk"] = jnp.concatenate([w[:, 2 * hq:], w[:, :2 * hq]], axis=1).astype(BF16)
            p["lg_fwd"] = ret_log_decay_fwd[j]
            p["lg_bwd"] = ret_log_decay_bwd[j]
            p["gn_gain"] = _row(ret_gn_gain[j])
            p["w_out"] = ret_w_out[j].astype(BF16)
        else:
            p["w_in"] = sgu_w_in[j].astype(BF16)
            p["b_in"] = _row(sgu_b_in[j])
            p["ln_g"] = sgu_ln_g[j].reshape(SGU_GROUPS, 1, SGU_GW)
            p["ln_b"] = sgu_ln_b[j].reshape(SGU_GROUPS, 1, SGU_GW)
            p["w_s"] = sgu_w_s[j].astype(BF16)
            p["b_s"] = sgu_b_s[j][:, :, None]
            p["w_out"] = sgu_w_out[j].astype(BF16)
        p["ffn_w_in"] = ffn_w_in[i].astype(BF16)
        p["ffn_conv"] = _conv_tiles(ffn_conv_w[i], ffn_conv_b[i])
        p["ffn_w_out"] = ffn_w_out[i].astype(BF16)
        layers.append(p)
    return layers, _row(ln_final)


def _trunk(x, layers, fin_gain, cos, sin):
    for i, p in enumerate(layers):
        if i % 2 == 0:
            vgqk = _ret_proj(x, p["ln_mix"], p["w_vgqk"], cos, sin)
            s_rev = _ret_state(vgqk, p["lg_bwd"])
            a = _retention(vgqk, s_rev, p["lg_fwd"], p["lg_bwd"], p["gn_gain"])
            x = _out_proj(a, p["w_out"], x)
        else:
            z, mean, rstd = _sgu_in(x, p["ln_mix"], p["w_in"], p["b_in"])
            x = _sgu_out(z, mean, rstd, p["ln_g"], p["ln_b"], p["w_s"], p["b_s"],
                         p["w_out"], x)
        x = _conv_ffn(x, p["ln_ffn"], p["ffn_w_in"], p["ffn_conv"], p["ffn_w_out"],
                      fin_gain, final_norm=(i == len(layers) - 1))
    return x


def kernel(x_prompt, x_sample, ln_mix, ln_ffn, ret_w_in, ret_log_decay_fwd, ret_log_decay_bwd, ret_gn_gain, ret_w_out, sgu_w_in, sgu_b_in, sgu_ln_g, sgu_ln_b, sgu_w_s, sgu_b_s, sgu_w_out, ffn_w_in, ffn_conv_w, ffn_conv_b, ffn_w_out, ln_final):
    layers, fin_gain = _prepare(
        ln_mix, ln_ffn, ret_w_in, ret_log_decay_fwd, ret_log_decay_bwd, ret_gn_gain,
        ret_w_out, sgu_w_in, sgu_b_in, sgu_ln_g, sgu_ln_b, sgu_w_s, sgu_b_s,
        sgu_w_out, ffn_w_in, ffn_conv_w, ffn_conv_b, ffn_w_out, ln_final)
    cos, sin = _rope_tables(max(x_prompt.shape[1], x_sample.shape[1]))
    return (_trunk(x_prompt, layers, fin_gain, cos, sin),
            _trunk(x_sample, layers, fin_gain, cos, sin))
```

```python
import functools

import jax
import jax.numpy as jnp
import numpy as np
from jax import lax
from jax.experimental import pallas as pl
from jax.experimental.pallas import tpu as pltpu

D_MODEL = 2048
RET_HEADS = 8
RET_DK = D_MODEL // RET_HEADS
RET_DV = 2 * D_MODEL // RET_HEADS
RET_HQ = RET_HEADS * RET_DK
RET_HV = RET_HEADS * RET_DV
RET_Q_BLK = 0
RET_K_BLK = 1
RET_V_BLK = 2 * RET_HQ // RET_HV
RET_G_BLK = RET_V_BLK + 1
ROPE_BASE = 10000.0
SGU_E = 3 * D_MODEL
SGU_GROUPS = 8
SGU_CHUNK = 128
SGU_GW = SGU_E // SGU_GROUPS
FFN_DIM = 5632
EPS = 1e-6
LN_EPS = 1e-5

SUBLANES_F32 = 8
LANES = 128
VMEM_LIMIT_BYTES = 60 * 1024 * 1024

PROJ_TM = 1024
PROJ_TN = 2048
PROJ_SUB = 512
RET_BLOCK = 256
OUT_TM = 512
SGU_TM = 512
FFN_TM = 1024
FFN_TF = 512
FFN_HALF = FFN_TF // 2
HALO = SUBLANES_F32

BF16 = jnp.bfloat16
F32 = jnp.float32


def _params(*semantics):
    return pltpu.CompilerParams(dimension_semantics=semantics,
                                vmem_limit_bytes=VMEM_LIMIT_BYTES)


def _rms_rows(x, gain):
    ms = jnp.mean(x * x, axis=-1, keepdims=True)
    return x * lax.rsqrt(ms + EPS) * gain


def _dot(a, b):
    return jnp.dot(a, b, preferred_element_type=F32)


def _silu(a):
    return a * jax.nn.sigmoid(a)


def _ret_proj_kernel(x_ref, gain_ref, w_ref, cos_ref, sin_ref, o_ref, hn_ref,
                     *, n_query, n_rope, n_plain):
    j = pl.program_id(2)

    @pl.when(j == 0)
    def _():
        hn_ref[...] = _rms_rows(x_ref[0], gain_ref[...]).astype(BF16)

    def store_plain(acc, lo):
        o_ref[0, :, lo:lo + PROJ_SUB] = acc.astype(BF16)

    def store_silu(acc, lo):
        o_ref[0, :, lo:lo + PROJ_SUB] = _silu(acc).astype(BF16)

    def store_rope(acc, lo):
        cos = cos_ref[...]
        sin = sin_ref[...]
        half = RET_DK // 2
        scale = jnp.where(j >= n_query, RET_DK ** -0.5, 1.0).astype(F32)
        for hh in range(PROJ_SUB // RET_DK):
            a = hh * RET_DK
            x1 = acc[:, a:a + half]
            x2 = acc[:, a + half:a + RET_DK]
            o_ref[0, :, lo + a:lo + a + half] = ((x1 * cos - x2 * sin) * scale).astype(BF16)
            o_ref[0, :, lo + a + half:lo + a + RET_DK] = ((x2 * cos + x1 * sin) * scale).astype(BF16)

    def chains(store):
        for s in range(PROJ_TN // PROJ_SUB):
            lo = s * PROJ_SUB
            store(_dot(hn_ref[...], w_ref[:, lo:lo + PROJ_SUB]), lo)

    @pl.when(j < n_rope)
    def _():
        chains(store_rope)

    @pl.when(jnp.logical_and(j >= n_rope, j < n_rope + n_plain))
    def _():
        chains(store_plain)

    @pl.when(j >= n_rope + n_plain)
    def _():
        chains(store_silu)


def _ret_proj(x, gain, w_qkvg, cos, sin):
    B, S, D = x.shape
    N = w_qkvg.shape[1]
    tm, tn = PROJ_TM, PROJ_TN
    half = RET_DK // 2
    kern = functools.partial(_ret_proj_kernel, n_query=RET_HQ // tn, n_rope=2 * RET_HQ // tn,
                             n_plain=RET_HV // tn)
    return pl.pallas_call(
        kern,
        grid=(B, S // tm, N // tn),
        in_specs=[
            pl.BlockSpec((1, tm, D), lambda b, i, j: (b, i, 0)),
            pl.BlockSpec((1, D), lambda b, i, j: (0, 0)),
            pl.BlockSpec((D, tn), lambda b, i, j: (0, j)),
            pl.BlockSpec((tm, half), lambda b, i, j: (i, 0)),
            pl.BlockSpec((tm, half), lambda b, i, j: (i, 0)),
        ],
        out_specs=pl.BlockSpec((1, tm, tn), lambda b, i, j: (b, i, j)),
        out_shape=jax.ShapeDtypeStruct((B, S, N), BF16),
        scratch_shapes=[pltpu.VMEM((tm, D), BF16)],
        compiler_params=_params("parallel", "parallel", "arbitrary"),
        name="ret_proj",
    )(x, gain, w_qkvg, cos, sin)


_A_BT = (((1,), (1,)), ((), ()))
_AT_B = (((0,), (0,)), ((), ()))


def _ret_state_kernel(lg_ref, k_ref, v_ref, s_ref, state_ref, kd_ref, cd_ref):
    C = RET_BLOCK
    t = pl.program_id(1)

    @pl.when(t == 0)
    def _():
        state_ref[...] = jnp.zeros_like(state_ref)
        krow = lax.broadcasted_iota(jnp.int32, (C, RET_DK), 0).astype(F32)
        for h in range(RET_HEADS):
            lg = lg_ref[h]
            kd_ref[h] = jnp.exp(lg * krow)
            cd_ref[h] = jnp.exp(jnp.full((1, RET_DV), lg * C, F32))

    for h in range(RET_HEADS):
        kcols = slice(h * RET_DK, (h + 1) * RET_DK)
        vcols = slice(h * RET_DV, (h + 1) * RET_DV)
        s_ref[0, 0, kcols, :] = state_ref[h].astype(BF16)
        ks = (k_ref[0, :, kcols].astype(F32) * kd_ref[h]).astype(BF16)
        state_ref[h] = (state_ref[h] * cd_ref[h]
                        + lax.dot_general(ks, v_ref[0, :, vcols], _AT_B,
                                          preferred_element_type=F32))


def _ret_state(qkvg, log_gamma):
    B, S, _ = qkvg.shape
    C = RET_BLOCK
    H = RET_HEADS
    nC = S // C
    return pl.pallas_call(
        _ret_state_kernel,
        grid=(B, nC),
        in_specs=[
            pl.BlockSpec(memory_space=pltpu.SMEM),
            pl.BlockSpec((1, C, RET_HQ), lambda b, t: (b, nC - 1 - t, RET_K_BLK)),
            pl.BlockSpec((1, C, RET_HV), lambda b, t: (b, nC - 1 - t, RET_V_BLK)),
        ],
        out_specs=pl.BlockSpec((1, 1, RET_HQ, RET_DV), lambda b, t: (b, nC - 1 - t, 0, 0)),
        out_shape=jax.ShapeDtypeStruct((B, nC, RET_HQ, RET_DV), BF16),
        scratch_shapes=[
            pltpu.VMEM((H, RET_DK, RET_DV), F32),
            pltpu.VMEM((H, C, RET_DK), F32),
            pltpu.VMEM((H, 1, RET_DV), F32),
        ],
        compiler_params=_params("parallel", "arbitrary"),
        name="retention_state",
    )(log_gamma, qkvg, qkvg)


def _retention_kernel(lgf_ref, lgb_ref, q_ref, k_ref, v_ref, srev_ref, gate_ref, gain_ref,
                      o_ref, state_ref, dec_ref, qdf_ref, qdb_ref, kd_ref, cd_ref):
    C = RET_BLOCK
    t = pl.program_id(1)

    @pl.when(t == 0)
    def _():
        state_ref[...] = jnp.zeros_like(state_ref)
        ri = lax.broadcasted_iota(jnp.int32, (C, C), 0)
        ci = lax.broadcasted_iota(jnp.int32, (C, C), 1)
        behind = jnp.maximum(ri - ci, 0).astype(F32)
        ahead = jnp.maximum(ci - ri, 0).astype(F32)
        qrow = lax.broadcasted_iota(jnp.int32, (C, RET_DV), 0).astype(F32)
        krow = lax.broadcasted_iota(jnp.int32, (C, RET_DK), 0).astype(F32)
        for h in range(RET_HEADS):
            lgf = lgf_ref[h]
            lgb = lgb_ref[h]
            dec_ref[h] = jnp.where(ri >= ci, jnp.exp(lgf * behind), jnp.exp(lgb * ahead))
            qdf_ref[h] = jnp.exp(lgf * (qrow + 1.0))
            qdb_ref[h] = jnp.exp(lgb * (C - qrow))
            kd_ref[h] = jnp.exp(lgf * (C - 1.0 - krow))
            cd_ref[h] = jnp.exp(jnp.full((1, RET_DV), lgf * C, F32))

    for h in range(RET_HEADS):
        kcols = slice(h * RET_DK, (h + 1) * RET_DK)
        vcols = slice(h * RET_DV, (h + 1) * RET_DV)
        q = q_ref[0, :, kcols]
        k = k_ref[0, :, kcols]
        v = v_ref[0, :, vcols]
        scores = lax.dot_general(q, k, _A_BT, preferred_element_type=F32) * dec_ref[h]
        y = _dot(scores.astype(BF16), v)
        y = y + qdf_ref[h] * _dot(q, state_ref[h].astype(BF16))
        y = y + qdb_ref[h] * _dot(q, srev_ref[0, 0, kcols, :])
        ks = (k.astype(F32) * kd_ref[h]).astype(BF16)
        state_ref[h] = (state_ref[h] * cd_ref[h]
                        + lax.dot_general(ks, v, _AT_B, preferred_element_type=F32))
        mu = jnp.mean(y, axis=-1, keepdims=True)
        yc = y - mu
        var = jnp.mean(yc * yc, axis=-1, keepdims=True)
        yn = yc * lax.rsqrt(var + LN_EPS) * gain_ref[:, vcols]
        o_ref[0, :, vcols] = (gate_ref[0, :, vcols].astype(F32) * yn).astype(BF16)


def _retention(qkvg, s_rev, lg_fwd, lg_bwd, gn_gain):
    B, S, _ = qkvg.shape
    C = RET_BLOCK
    H = RET_HEADS
    smem = pl.BlockSpec(memory_space=pltpu.SMEM)
    return pl.pallas_call(
        _retention_kernel,
        grid=(B, S // C),
        in_specs=[
            smem,
            smem,
            pl.BlockSpec((1, C, RET_HQ), lambda b, t: (b, t, RET_Q_BLK)),
            pl.BlockSpec((1, C, RET_HQ), lambda b, t: (b, t, RET_K_BLK)),
            pl.BlockSpec((1, C, RET_HV), lambda b, t: (b, t, RET_V_BLK)),
            pl.BlockSpec((1, 1, RET_HQ, RET_DV), lambda b, t: (b, t, 0, 0)),
            pl.BlockSpec((1, C, RET_HV), lambda b, t: (b, t, RET_G_BLK)),
            pl.BlockSpec((1, RET_HV), lambda b, t: (0, 0)),
        ],
        out_specs=pl.BlockSpec((1, C, RET_HV), lambda b, t: (b, t, 0)),
        out_shape=jax.ShapeDtypeStruct((B, S, RET_HV), BF16),
        scratch_shapes=[
            pltpu.VMEM((H, RET_DK, RET_DV), F32),
            pltpu.VMEM((H, C, C), F32),
            pltpu.VMEM((H, C, RET_DV), F32),
            pltpu.VMEM((H, C, RET_DV), F32),
            pltpu.VMEM((H, C, RET_DK), F32),
            pltpu.VMEM((H, 1, RET_DV), F32),
        ],
        compiler_params=_params("parallel", "arbitrary"),
        name="retention",
    )(lg_fwd, lg_bwd, qkvg, qkvg, qkvg, s_rev, qkvg, gn_gain)


def _out_proj_kernel(a_ref, w_ref, x_ref, o_ref):
    N = w_ref.shape[1]
    for s in range(N // PROJ_SUB):
        cols = slice(s * PROJ_SUB, (s + 1) * PROJ_SUB)
        o_ref[0, :, cols] = x_ref[0, :, cols] + _dot(a_ref[0], w_ref[:, cols])


def _out_proj(a, w, x):
    B, S, K = a.shape
    N = w.shape[1]
    tm = OUT_TM
    return pl.pallas_call(
        _out_proj_kernel,
        grid=(B, S // tm),
        in_specs=[
            pl.BlockSpec((1, tm, K), lambda b, i: (b, i, 0)),
            pl.BlockSpec((K, N), lambda b, i: (0, 0), pipeline_mode=pl.Buffered(1)),
            pl.BlockSpec((1, tm, N), lambda b, i: (b, i, 0)),
        ],
        out_specs=pl.BlockSpec((1, tm, N), lambda b, i: (b, i, 0)),
        out_shape=jax.ShapeDtypeStruct((B, S, N), F32),
        compiler_params=_params("parallel", "parallel"),
        name="out_proj",
    )(a, w, x)


def _sgu_in_kernel(x_ref, gain_ref, w_ref, b_ref, z_ref, mean_ref, rstd_ref,
                   hn_ref, s1_ref, s2_ref, *, n_u, n_tiles):
    j = pl.program_id(2)

    @pl.when(j == 0)
    def _():
        hn_ref[...] = _rms_rows(x_ref[0], gain_ref[...]).astype(BF16)
        s1_ref[...] = jnp.zeros_like(s1_ref)
        s2_ref[...] = jnp.zeros_like(s2_ref)

    def chains(with_stats):
        for s in range(PROJ_TN // PROJ_SUB):
            cols = slice(s * PROJ_SUB, (s + 1) * PROJ_SUB)
            acc = _dot(hn_ref[...], w_ref[:, cols]) + b_ref[:, cols]
            z = 0.5 * acc * (1.0 + lax.erf(acc * np.float32(np.sqrt(0.5))))
            z_ref[0, :, cols] = z.astype(BF16)
            if with_stats:
                zz = z * z
                p1 = z[:, 0:LANES]
                p2 = zz[:, 0:LANES]
                for c in range(1, PROJ_SUB // LANES):
                    p1 = p1 + z[:, c * LANES:(c + 1) * LANES]
                    p2 = p2 + zz[:, c * LANES:(c + 1) * LANES]
                s1_ref[...] += p1
                s2_ref[...] += p2

    @pl.when(j < n_u)
    def _():
        chains(False)

    @pl.when(j >= n_u)
    def _():
        chains(True)

    @pl.when(j == n_tiles - 1)
    def _():
        mean = jnp.sum(s1_ref[...], axis=-1, keepdims=True) * (1.0 / SGU_E)
        var = jnp.sum(s2_ref[...], axis=-1, keepdims=True) * (1.0 / SGU_E) - mean * mean
        mean_ref[0] = mean
        rstd_ref[0] = lax.rsqrt(var + LN_EPS)


def _sgu_in(x, gain, w, b):
    B, S, D = x.shape
    N = w.shape[1]
    tm, tn = PROJ_TM, PROJ_TN
    n_tiles = N // tn
    kern = functools.partial(_sgu_in_kernel, n_u=SGU_E // tn, n_tiles=n_tiles)
    return pl.pallas_call(
        kern,
        grid=(B, S // tm, n_tiles),
        in_specs=[
            pl.BlockSpec((1, tm, D), lambda b_, i, j: (b_, i, 0)),
            pl.BlockSpec((1, D), lambda b_, i, j: (0, 0)),
            pl.BlockSpec((D, tn), lambda b_, i, j: (0, j)),
            pl.BlockSpec((1, tn), lambda b_, i, j: (0, j)),
        ],
        out_specs=[
            pl.BlockSpec((1, tm, tn), lambda b_, i, j: (b_, i, j)),
            pl.BlockSpec((1, tm, 1), lambda b_, i, j: (b_, i, 0)),
            pl.BlockSpec((1, tm, 1), lambda b_, i, j: (b_, i, 0)),
        ],
        out_shape=[
            jax.ShapeDtypeStruct((B, S, N), BF16),
            jax.ShapeDtypeStruct((B, S, 1), F32),
            jax.ShapeDtypeStruct((B, S, 1), F32),
        ],
        scratch_shapes=[
            pltpu.VMEM((tm, D), BF16),
            pltpu.VMEM((tm, LANES), F32),
            pltpu.VMEM((tm, LANES), F32),
        ],
        compiler_params=_params("parallel", "parallel", "arbitrary"),
        name="sgu_in",
    )(x, gain, w, b)


def _sgu_out_kernel(u_ref, v_ref, mean_ref, rstd_ref, lng_ref, lnb_ref, ws_ref,
                    bs_ref, wo_prev_ref, wo_ref, wo_tail_ref, x_ref, o_ref, acc_ref, ga_ref,
                    gb_ref,
                    *, n_pairs):
    s = pl.program_id(2)
    gw = SGU_GW

    @pl.when(s == 0)
    def _():
        acc_ref[...] = jnp.zeros_like(acc_ref)
        gb_ref[...] = jnp.zeros_like(gb_ref)

    def gate_group(k, out_ref):
        g = 2 * s + k
        cols = slice(k * gw, (k + 1) * gw)
        vn = ((v_ref[0, :, cols].astype(F32) - mean_ref[0]) * rstd_ref[0] * lng_ref[g]
              + lnb_ref[g]).astype(BF16)
        ws = ws_ref[g]
        bs = bs_ref[g]
        for c in range(SGU_TM // SGU_CHUNK):
            rows = slice(c * SGU_CHUNK, (c + 1) * SGU_CHUNK)
            mixed = _dot(ws, vn[rows]) + bs
            out_ref[rows, :] = (u_ref[0, rows, cols].astype(F32) * mixed).astype(BF16)

    acc_ref[...] += _dot(gb_ref[...], wo_prev_ref[...])
    gate_group(0, ga_ref)
    acc_ref[...] += _dot(ga_ref[...], wo_ref[...])
    gate_group(1, gb_ref)

    @pl.when(s == n_pairs - 1)
    def _():
        o_ref[0] = x_ref[0] + acc_ref[...] + _dot(gb_ref[...], wo_tail_ref[...])


def _sgu_out(z, mean, rstd, ln_g, ln_b, w_s, b_s, w_out, x):
    B, S, D = x.shape
    tm = SGU_TM
    G = SGU_GROUPS
    gw = SGU_GW
    n_pairs = G // 2
    kern = functools.partial(_sgu_out_kernel, n_pairs=n_pairs)
    whole = lambda a: pl.BlockSpec(a.shape, lambda b, i, s: (0,) * a.ndim)
    return pl.pallas_call(
        kern,
        grid=(B, S // tm, n_pairs),
        in_specs=[
            pl.BlockSpec((1, tm, 2 * gw), lambda b, i, s: (b, i, s)),
            pl.BlockSpec((1, tm, 2 * gw), lambda b, i, s: (b, i, n_pairs + s)),
            pl.BlockSpec((1, tm, 1), lambda b, i, s: (b, i, 0)),
            pl.BlockSpec((1, tm, 1), lambda b, i, s: (b, i, 0)),
            whole(ln_g),
            whole(ln_b),
            whole(w_s),
            whole(b_s),
            pl.BlockSpec((gw, D), lambda b, i, s: (jnp.maximum(2 * s - 1, 0), 0)),
            pl.BlockSpec((gw, D), lambda b, i, s: (2 * s, 0)),
            pl.BlockSpec((gw, D), lambda b, i, s: (G - 1, 0)),
            pl.BlockSpec((1, tm, D), lambda b, i, s: (b, i, 0)),
        ],
        out_specs=pl.BlockSpec((1, tm, D), lambda b, i, s: (b, i, 0)),
        out_shape=jax.ShapeDtypeStruct((B, S, D), F32),
        scratch_shapes=[
            pltpu.VMEM((tm, D), F32),
            pltpu.VMEM((tm, gw), BF16),
            pltpu.VMEM((tm, gw), BF16),
        ],
        compiler_params=_params("parallel", "parallel", "arbitrary"),
        name="sgu_out",
    )(z, z, mean, rstd, ln_g, ln_b, w_s, b_s, w_out, w_out, w_out, x)


def _stage(a, scr_ref, first):
    tm = FFN_TM
    for c in range(FFN_HALF // LANES):
        blk = a[:, c * LANES:(c + 1) * LANES]
        scr_ref[first + c, 0:HALO, :] = blk[tm + HALO:tm + 2 * HALO]
        scr_ref[first + c, HALO:tm + 2 * HALO, :] = blk[:tm + HALO]


def _conv3(scr_ref, slab, cw, cb):
    tm = FFN_TM
    prev = scr_ref[slab, HALO - 1:HALO - 1 + tm, :]
    cur = scr_ref[slab, HALO:HALO + tm, :]
    nxt = scr_ref[slab, HALO + 1:HALO + 1 + tm, :]
    return prev * cw[0:1, :] + cur * cw[1:2, :] + nxt * cw[2:3, :] + cb


def _conv_ffn_kernel(x_hbm, xprev_ref, xnext_ref, gain_ref, wg_ref, wu_ref, cw_ref,
                     wo_ref, fin_ref, o_ref, xs_ref, xs_sem, hn_ref, acta_ref, actb_ref,
                     sa_ref, sb_ref, *, n_row_tiles, n_f, final_norm):
    b = pl.program_id(0)
    i = pl.program_id(1)
    j = pl.program_id(2)
    tm = FFN_TM
    nblk = FFN_HALF // LANES
    up0 = FFN_TF
    n_tiles = pl.num_programs(0) * n_row_tiles

    def x_copy(bb, ii):
        return pltpu.make_async_copy(x_hbm.at[bb, pl.ds(ii * tm, tm), :], xs_ref, xs_sem)

    @pl.when(j == 0)
    def _():
        @pl.when(jnp.logical_and(b == 0, i == 0))
        def _():
            x_copy(b, i).start()

        x_copy(b, i).wait()
        gain = gain_ref[...]
        hn_ref[0:tm, :] = _rms_rows(xs_ref[...], gain).astype(BF16)
        has_next = (i < n_row_tiles - 1).astype(F32)
        has_prev = (i > 0).astype(F32)
        halo = jnp.concatenate([_rms_rows(xnext_ref[0], gain) * has_next,
                                _rms_rows(xprev_ref[0], gain) * has_prev], axis=0)
        hn_ref[tm:tm + 2 * HALO, :] = halo.astype(BF16)
        o_ref[0] = xs_ref[...]

    @pl.when(j == 1)
    def _():
        nxt = b * n_row_tiles + i + 1

        @pl.when(nxt < n_tiles)
        def _():
            x_copy(nxt // n_row_tiles, nxt % n_row_tiles).start()

    def in_proj(lo, scr_ref):
        hn = hn_ref[...]
        _stage(_dot(hn, wg_ref[:, lo:lo + FFN_HALF]), scr_ref, 0)
        _stage(_dot(hn, wu_ref[:, lo:lo + FFN_HALF]), scr_ref, nblk)

    def conv_act(scr_ref, lo, act_ref):
        cw = cw_ref[j]
        for c in range(nblk):
            g0 = lo + c * LANES
            u0 = up0 + g0
            gate = _conv3(scr_ref, c, cw[0:3, g0:g0 + LANES], cw[3:4, g0:g0 + LANES])
            up = _conv3(scr_ref, nblk + c, cw[0:3, u0:u0 + LANES], cw[3:4, u0:u0 + LANES])
            act_ref[:, c * LANES:(c + 1) * LANES] = (_silu(gate) * up).astype(BF16)

    in_proj(0, sa_ref)
    in_proj(FFN_HALF, sb_ref)
    conv_act(sa_ref, 0, acta_ref)
    o_ref[0] += _dot(acta_ref[...], wo_ref[0:FFN_HALF, :])
    conv_act(sb_ref, FFN_HALF, actb_ref)
    o_ref[0] += _dot(actb_ref[...], wo_ref[FFN_HALF:FFN_TF, :])

    if final_norm:
        @pl.when(j == n_f - 1)
        def _():
            o_ref[0] = _rms_rows(o_ref[0], fin_ref[...])


def _conv_ffn(x, gain, w_in, conv_tiles, w_out, fin_gain, *, final_norm):
    B, S, D = x.shape
    tm, tf = FFN_TM, FFN_TF
    n_f = FFN_DIM // tf
    n_row_tiles = S // tm
    hb = tm // HALO
    last_hb = S // HALO - 1
    kern = functools.partial(_conv_ffn_kernel, n_row_tiles=n_row_tiles, n_f=n_f,
                             final_norm=final_norm)
    return pl.pallas_call(
        kern,
        grid=(B, n_row_tiles, n_f),
        in_specs=[
            pl.BlockSpec(memory_space=pl.ANY),
            pl.BlockSpec((1, HALO, D),
                         lambda b, i, j: (b, jnp.maximum(i * hb - 1, 0), 0)),
            pl.BlockSpec((1, HALO, D),
                         lambda b, i, j: (b, jnp.minimum((i + 1) * hb, last_hb), 0)),
            pl.BlockSpec((1, D), lambda b, i, j: (0, 0)),
            pl.BlockSpec((D, tf), lambda b, i, j: (0, j)),
            pl.BlockSpec((D, tf), lambda b, i, j: (0, n_f + j)),
            pl.BlockSpec(conv_tiles.shape, lambda b, i, j: (0, 0, 0)),
            pl.BlockSpec((tf, D), lambda b, i, j: (j, 0)),
            pl.BlockSpec((1, D), lambda b, i, j: (0, 0)),
        ],
        out_specs=pl.BlockSpec((1, tm, D), lambda b, i, j: (b, i, 0)),
        out_shape=jax.ShapeDtypeStruct((B, S, D), F32),
        scratch_shapes=[
            pltpu.VMEM((tm, D), F32),
            pltpu.SemaphoreType.DMA(()),
            pltpu.VMEM((tm + 2 * HALO, D), BF16),
            pltpu.VMEM((tm, FFN_HALF), BF16),
            pltpu.VMEM((tm, FFN_HALF), BF16),
            pltpu.VMEM((2 * FFN_HALF // LANES, tm + 2 * HALO, LANES), F32),
            pltpu.VMEM((2 * FFN_HALF // LANES, tm + 2 * HALO, LANES), F32),
        ],
        compiler_params=_params("arbitrary", "arbitrary", "arbitrary"),
        name="conv_ffn",
    )(x, x, x, gain, w_in, w_in, conv_tiles, w_out, fin_gain)


def _conv_tiles(conv_w, conv_b):
    n_f = FFN_DIM // FFN_TF
    a = jnp.concatenate([conv_w, conv_b.reshape(1, -1)], axis=0)
    a = a.reshape(4, 2, n_f, FFN_TF)
    return a.transpose(2, 0, 1, 3).reshape(n_f, 4, 2 * FFN_TF)


def _rope_tables(S):
    half = RET_DK // 2
    inv = ROPE_BASE ** (-jnp.arange(half, dtype=F32) / half)
    ang = jnp.arange(S, dtype=F32)[:, None] * inv[None, :]
    return jnp.cos(ang), jnp.sin(ang)


def _row(v):
    return v.reshape(1, -1)


def _prepare(ln_mix, ln_ffn, ret_w_in, ret_log_decay_fwd, ret_log_decay_bwd,
             ret_gn_gain, ret_w_out, sgu_w_in, sgu_b_in, sgu_ln_g, sgu_ln_b,
             sgu_w_s, sgu_b_s, sgu_w_out, ffn_w_in, ffn_conv_w, ffn_conv_b,
             ffn_w_out, ln_final):
    depth = ln_mix.shape[0]
    layers = []
    for i in range(depth):
        j = i // 2
        p = {"ln_mix": _row(ln_mix[i]), "ln_ffn": _row(ln_ffn[i])}
        if i % 2 == 0:
            p["w_qkvg"] = ret_w_in[j].astype(BF16)
            p["lg_fwd"] = ret_log_decay_fwd[j]
            p["lg_bwd"] = ret_log_decay_bwd[j]
            p["gn_gain"] = _row(ret_gn_gain[j])
            p["w_out"] = ret_w_out[j].astype(BF16)
        else:
            p["w_in"] = sgu_w_in[j].astype(BF16)
            p["b_in"] = _row(sgu_b_in[j])
            p["ln_g"] = sgu_ln_g[j].reshape(SGU_GROUPS, 1, SGU_GW)
            p["ln_b"] = sgu_ln_b[j].reshape(SGU_GROUPS, 1, SGU_GW)
            p["w_s"] = sgu_w_s[j].astype(BF16)
            p["b_s"] = sgu_b_s[j][:, :, None]
            p["w_out"] = sgu_w_out[j].astype(BF16)
        p["ffn_w_in"] = ffn_w_in[i].astype(BF16)
        p["ffn_conv"] = _conv_tiles(ffn_conv_w[i], ffn_conv_b[i])
        p["ffn_w_out"] = ffn_w_out[i].astype(BF16)
        layers.append(p)
    return layers, _row(ln_final)


def _trunk(x, layers, fin_gain, cos, sin):
    for i, p in enumerate(layers):
        if i % 2 == 0:
            qkvg = _ret_proj(x, p["ln_mix"], p["w_qkvg"], cos, sin)
            s_rev = _ret_state(qkvg, p["lg_bwd"])
            a = _retention(qkvg, s_rev, p["lg_fwd"], p["lg_bwd"], p["gn_gain"])
            x = _out_proj(a, p["w_out"], x)
        else:
            z, mean, rstd = _sgu_in(x, p["ln_mix"], p["w_in"], p["b_in"])
            x = _sgu_out(z, mean, rstd, p["ln_g"], p["ln_b"], p["w_s"], p["b_s"],
                         p["w_out"], x)
        x = _conv_ffn(x, p["ln_ffn"], p["ffn_w_in"], p["ffn_conv"], p["ffn_w_out"],
                      fin_gain, final_norm=(i == len(layers) - 1))
    return x


def kernel(x_prompt, x_sample, ln_mix, ln_ffn, ret_w_in, ret_log_decay_fwd, ret_log_decay_bwd, ret_gn_gain, ret_w_out, sgu_w_in, sgu_b_in, sgu_ln_g, sgu_ln_b, sgu_w_s, sgu_b_s, sgu_w_out, ffn_w_in, ffn_conv_w, ffn_conv_b, ffn_w_out, ln_final):
    layers, fin_gain = _prepare(
        ln_mix, ln_ffn, ret_w_in, ret_log_decay_fwd, ret_log_decay_bwd, ret_gn_gain,
        ret_w_out, sgu_w_in, sgu_b_in, sgu_ln_g, sgu_ln_b, sgu_w_s, sgu_b_s,
        sgu_w_out, ffn_w_in, ffn_conv_w, ffn_conv_b, ffn_w_out, ln_final)
    cos, sin = _rope_tables(max(x_prompt.shape[1], x_sample.shape[1]))
    return (_trunk(x_prompt, layers, fin_gain, cos, sin),
            _trunk(x_sample, layers, fin_gain, cos, sin))
```

```python
import functools

import jax
import jax.numpy as jnp
import numpy as np
from jax import lax
from jax.experimental import pallas as pl
from jax.experimental.pallas import tpu as pltpu

D_MODEL = 2048
RET_HEADS = 8
RET_DK = D_MODEL // RET_HEADS
RET_DV = 2 * D_MODEL // RET_HEADS
RET_HQ = RET_HEADS * RET_DK
RET_HV = RET_HEADS * RET_DV
RET_Q_BLK = 0
RET_K_BLK = 1
RET_V_BLK = 2 * RET_HQ // RET_HV
RET_G_BLK = RET_V_BLK + 1
ROPE_BASE = 10000.0
SGU_E = 3 * D_MODEL
SGU_GROUPS = 8
SGU_CHUNK = 128
SGU_GW = SGU_E // SGU_GROUPS
FFN_DIM = 5632
EPS = 1e-6
LN_EPS = 1e-5

SUBLANES_F32 = 8
LANES = 128
VMEM_LIMIT_BYTES = 60 * 1024 * 1024

PROJ_TM = 1024
PROJ_TN = 2048
PROJ_SUB = 512
RET_BLOCK = 256
OUT_TM = 512
SGU_TM = 512
FFN_TM = 1024
FFN_TF = 512
FFN_HALF = FFN_TF // 2
HALO = SUBLANES_F32

BF16 = jnp.bfloat16
F32 = jnp.float32


def _params(*semantics):
    return pltpu.CompilerParams(dimension_semantics=semantics,
                                vmem_limit_bytes=VMEM_LIMIT_BYTES)


def _rms_rows(x, gain):
    ms = jnp.mean(x * x, axis=-1, keepdims=True)
    return x * lax.rsqrt(ms + EPS) * gain


def _dot(a, b):
    return jnp.dot(a, b, preferred_element_type=F32)


def _silu(a):
    return a * jax.nn.sigmoid(a)


def _ret_proj_kernel(x_ref, gain_ref, w_ref, cos_ref, sin_ref, o_ref, hn_ref,
                     *, n_query, n_rope, n_plain):
    j = pl.program_id(2)

    @pl.when(j == 0)
    def _():
        hn_ref[...] = _rms_rows(x_ref[0], gain_ref[...]).astype(BF16)

    def store_plain(acc, lo):
        o_ref[0, :, lo:lo + PROJ_SUB] = acc.astype(BF16)

    def store_silu(acc, lo):
        o_ref[0, :, lo:lo + PROJ_SUB] = _silu(acc).astype(BF16)

    def store_rope(acc, lo):
        cos = cos_ref[...]
        sin = sin_ref[...]
        half = RET_DK // 2
        scale = jnp.where(j >= n_query, RET_DK ** -0.5, 1.0).astype(F32)
        for hh in range(PROJ_SUB // RET_DK):
            a = hh * RET_DK
            x1 = acc[:, a:a + half]
            x2 = acc[:, a + half:a + RET_DK]
            o_ref[0, :, lo + a:lo + a + half] = ((x1 * cos - x2 * sin) * scale).astype(BF16)
            o_ref[0, :, lo + a + half:lo + a + RET_DK] = ((x2 * cos + x1 * sin) * scale).astype(BF16)

    def chains(store):
        for s in range(PROJ_TN // PROJ_SUB):
            lo = s * PROJ_SUB
            store(_dot(hn_ref[...], w_ref[:, lo:lo + PROJ_SUB]), lo)

    @pl.when(j < n_rope)
    def _():
        chains(store_rope)

    @pl.when(jnp.logical_and(j >= n_rope, j < n_rope + n_plain))
    def _():
        chains(store_plain)

    @pl.when(j >= n_rope + n_plain)
    def _():
        chains(store_silu)


def _ret_proj(x, gain, w_qkvg, cos, sin):
    B, S, D = x.shape
    N = w_qkvg.shape[1]
    tm, tn = PROJ_TM, PROJ_TN
    half = RET_DK // 2
    kern = functools.partial(_ret_proj_kernel, n_query=RET_HQ // tn, n_rope=2 * RET_HQ // tn,
                             n_plain=RET_HV // tn)
    return pl.pallas_call(
        kern,
        grid=(B, S // tm, N // tn),
        in_specs=[
            pl.BlockSpec((1, tm, D), lambda b, i, j: (b, i, 0)),
            pl.BlockSpec((1, D), lambda b, i, j: (0, 0)),
            pl.BlockSpec((D, tn), lambda b, i, j: (0, j)),
            pl.BlockSpec((tm, half), lambda b, i, j: (i, 0)),
            pl.BlockSpec((tm, half), lambda b, i, j: (i, 0)),
        ],
        out_specs=pl.BlockSpec((1, tm, tn), lambda b, i, j: (b, i, j)),
        out_shape=jax.ShapeDtypeStruct((B, S, N), BF16),
        scratch_shapes=[pltpu.VMEM((tm, D), BF16)],
        compiler_params=_params("parallel", "parallel", "arbitrary"),
        name="ret_proj",
    )(x, gain, w_qkvg, cos, sin)


_A_BT = (((1,), (1,)), ((), ()))
_AT_B = (((0,), (0,)), ((), ()))


def _ret_state_kernel(lg_ref, k_ref, v_ref, s_ref, state_ref, kd_ref, cd_ref):
    C = RET_BLOCK
    t = pl.program_id(1)

    @pl.when(t == 0)
    def _():
        state_ref[...] = jnp.zeros_like(state_ref)
        krow = lax.broadcasted_iota(jnp.int32, (C, RET_DK), 0).astype(F32)
        for h in range(RET_HEADS):
            lg = lg_ref[h]
            kd_ref[h] = jnp.exp(lg * krow)
            cd_ref[h] = jnp.exp(jnp.full((1, RET_DV), lg * C, F32))

    for h in range(RET_HEADS):
        kcols = slice(h * RET_DK, (h + 1) * RET_DK)
        vcols = slice(h * RET_DV, (h + 1) * RET_DV)
        s_ref[0, 0, kcols, :] = state_ref[h].astype(BF16)
        ks = (k_ref[0, :, kcols].astype(F32) * kd_ref[h]).astype(BF16)
        state_ref[h] = (state_ref[h] * cd_ref[h]
                        + lax.dot_general(ks, v_ref[0, :, vcols], _AT_B,
                                          preferred_element_type=F32))


def _ret_state(qkvg, log_gamma):
    B, S, _ = qkvg.shape
    C = RET_BLOCK
    H = RET_HEADS
    nC = S // C
    return pl.pallas_call(
        _ret_state_kernel,
        grid=(B, nC),
        in_specs=[
            pl.BlockSpec(memory_space=pltpu.SMEM),
            pl.BlockSpec((1, C, RET_HQ), lambda b, t: (b, nC - 1 - t, RET_K_BLK)),
            pl.BlockSpec((1, C, RET_HV), lambda b, t: (b, nC - 1 - t, RET_V_BLK)),
        ],
        out_specs=pl.BlockSpec((1, 1, RET_HQ, RET_DV), lambda b, t: (b, nC - 1 - t, 0, 0)),
        out_shape=jax.ShapeDtypeStruct((B, nC, RET_HQ, RET_DV), BF16),
        scratch_shapes=[
            pltpu.VMEM((H, RET_DK, RET_DV), F32),
            pltpu.VMEM((H, C, RET_DK), F32),
            pltpu.VMEM((H, 1, RET_DV), F32),
        ],
        compiler_params=_params("parallel", "arbitrary"),
        name="retention_state",
    )(log_gamma, qkvg, qkvg)


def _retention_kernel(lgf_ref, lgb_ref, q_ref, k_ref, v_ref, srev_ref, gate_ref, gain_ref,
                      o_ref, state_ref, dec_ref, qdf_ref, qdb_ref, kd_ref, cd_ref):
    C = RET_BLOCK
    t = pl.program_id(1)

    @pl.when(t == 0)
    def _():
        state_ref[...] = jnp.zeros_like(state_ref)
        ri = lax.broadcasted_iota(jnp.int32, (C, C), 0)
        ci = lax.broadcasted_iota(jnp.int32, (C, C), 1)
        behind = jnp.maximum(ri - ci, 0).astype(F32)
        ahead = jnp.maximum(ci - ri, 0).astype(F32)
        qrow = lax.broadcasted_iota(jnp.int32, (C, RET_DV), 0).astype(F32)
        krow = lax.broadcasted_iota(jnp.int32, (C, RET_DK), 0).astype(F32)
        for h in range(RET_HEADS):
            lgf = lgf_ref[h]
            lgb = lgb_ref[h]
            dec_ref[h] = jnp.where(ri >= ci, jnp.exp(lgf * behind), jnp.exp(lgb * ahead))
            qdf_ref[h] = jnp.exp(lgf * (qrow + 1.0))
            qdb_ref[h] = jnp.exp(lgb * (C - qrow))
            kd_ref[h] = jnp.exp(lgf * (C - 1.0 - krow))
            cd_ref[h] = jnp.exp(jnp.full((1, RET_DV), lgf * C, F32))

    for h in range(RET_HEADS):
        kcols = slice(h * RET_DK, (h + 1) * RET_DK)
        vcols = slice(h * RET_DV, (h + 1) * RET_DV)
        q = q_ref[0, :, kcols]
        k = k_ref[0, :, kcols]
        v = v_ref[0, :, vcols]
        scores = lax.dot_general(q, k, _A_BT, preferred_element_type=F32) * dec_ref[h]
        y = _dot(scores.astype(BF16), v)
        y = y + qdf_ref[h] * _dot(q, state_ref[h].astype(BF16))
        y = y + qdb_ref[h] * _dot(q, srev_ref[0, 0, kcols, :])
        ks = (k.astype(F32) * kd_ref[h]).astype(BF16)
        state_ref[h] = (state_ref[h] * cd_ref[h]
                        + lax.dot_general(ks, v, _AT_B, preferred_element_type=F32))
        mu = jnp.mean(y, axis=-1, keepdims=True)
        yc = y - mu
        var = jnp.mean(yc * yc, axis=-1, keepdims=True)
        yn = yc * lax.rsqrt(var + LN_EPS) * gain_ref[:, vcols]
        o_ref[0, :, vcols] = (gate_ref[0, :, vcols].astype(F32) * yn).astype(BF16)


def _retention(qkvg, s_rev, lg_fwd, lg_bwd, gn_gain):
    B, S, _ = qkvg.shape
    C = RET_BLOCK
    H = RET_HEADS
    smem = pl.BlockSpec(memory_space=pltpu.SMEM)
    return pl.pallas_call(
        _retention_kernel,
        grid=(B, S // C),
        in_specs=[
            smem,
            smem,
            pl.BlockSpec((1, C, RET_HQ), lambda b, t: (b, t, RET_Q_BLK)),
            pl.BlockSpec((1, C, RET_HQ), lambda b, t: (b, t, RET_K_BLK)),
            pl.BlockSpec((1, C, RET_HV), lambda b, t: (b, t, RET_V_BLK)),
            pl.BlockSpec((1, 1, RET_HQ, RET_DV), lambda b, t: (b, t, 0, 0)),
            pl.BlockSpec((1, C, RET_HV), lambda b, t: (b, t, RET_G_BLK)),
            pl.BlockSpec((1, RET_HV), lambda b, t: (0, 0)),
        ],
        out_specs=pl.BlockSpec((1, C, RET_HV), lambda b, t: (b, t, 0)),
        out_shape=jax.ShapeDtypeStruct((B, S, RET_HV), BF16),
        scratch_shapes=[
            pltpu.VMEM((H, RET_DK, RET_DV), F32),
            pltpu.VMEM((H, C, C), F32),
            pltpu.VMEM((H, C, RET_DV), F32),
            pltpu.VMEM((H, C, RET_DV), F32),
            pltpu.VMEM((H, C, RET_DK), F32),
            pltpu.VMEM((H, 1, RET_DV), F32),
        ],
        compiler_params=_params("parallel", "arbitrary"),
        name="retention",
    )(lg_fwd, lg_bwd, qkvg, qkvg, qkvg, s_rev, qkvg, gn_gain)


def _out_proj_kernel(a_ref, w_ref, x_ref, o_ref):
    N = w_ref.shape[1]
    for s in range(N // PROJ_SUB):
        cols = slice(s * PROJ_SUB, (s + 1) * PROJ_SUB)
        o_ref[0, :, cols] = x_ref[0, :, cols] + _dot(a_ref[0], w_ref[:, cols])


def _out_proj(a, w, x):
    B, S, K = a.shape
    N = w.shape[1]
    tm = OUT_TM
    return pl.pallas_call(
        _out_proj_kernel,
        grid=(B, S // tm),
        in_specs=[
            pl.BlockSpec((1, tm, K), lambda b, i: (b, i, 0)),
            pl.BlockSpec((K, N), lambda b, i: (0, 0), pipeline_mode=pl.Buffered(1)),
            pl.BlockSpec((1, tm, N), lambda b, i: (b, i, 0)),
        ],
        out_specs=pl.BlockSpec((1, tm, N), lambda b, i: (b, i, 0)),
        out_shape=jax.ShapeDtypeStruct((B, S, N), F32),
        compiler_params=_params("parallel", "parallel"),
        name="out_proj",
    )(a, w, x)


def _sgu_in_kernel(x_ref, gain_ref, w_ref, b_ref, z_ref, mean_ref, rstd_ref,
                   hn_ref, s1_ref, s2_ref, *, n_u, n_tiles):
    j = pl.program_id(2)

    @pl.when(j == 0)
    def _():
        hn_ref[...] = _rms_rows(x_ref[0], gain_ref[...]).astype(BF16)
        s1_ref[...] = jnp.zeros_like(s1_ref)
        s2_ref[...] = jnp.zeros_like(s2_ref)

    def chains(with_stats):
        for s in range(PROJ_TN // PROJ_SUB):
            cols = slice(s * PROJ_SUB, (s + 1) * PROJ_SUB)
            acc = _dot(hn_ref[...], w_ref[:, cols]) + b_ref[:, cols]
            z = 0.5 * acc * (1.0 + lax.erf(acc * np.float32(np.sqrt(0.5))))
            z_ref[0, :, cols] = z.astype(BF16)
            if with_stats:
                zz = z * z
                p1 = z[:, 0:LANES]
                p2 = zz[:, 0:LANES]
                for c in range(1, PROJ_SUB // LANES):
                    p1 = p1 + z[:, c * LANES:(c + 1) * LANES]
                    p2 = p2 + zz[:, c * LANES:(c + 1) * LANES]
                s1_ref[...] += p1
                s2_ref[...] += p2

    @pl.when(j < n_u)
    def _():
        chains(False)

    @pl.when(j >= n_u)
    def _():
        chains(True)

    @pl.when(j == n_tiles - 1)
    def _():
        mean = jnp.sum(s1_ref[...], axis=-1, keepdims=True) * (1.0 / SGU_E)
        var = jnp.sum(s2_ref[...], axis=-1, keepdims=True) * (1.0 / SGU_E) - mean * mean
        mean_ref[0] = mean
        rstd_ref[0] = lax.rsqrt(var + LN_EPS)


def _sgu_in(x, gain, w, b):
    B, S, D = x.shape
    N = w.shape[1]
    tm, tn = PROJ_TM, PROJ_TN
    n_tiles = N // tn
    kern = functools.partial(_sgu_in_kernel, n_u=SGU_E // tn, n_tiles=n_tiles)
    return pl.pallas_call(
        kern,
        grid=(B, S // tm, n_tiles),
        in_specs=[
            pl.BlockSpec((1, tm, D), lambda b_, i, j: (b_, i, 0)),
            pl.BlockSpec((1, D), lambda b_, i, j: (0, 0)),
            pl.BlockSpec((D, tn), lambda b_, i, j: (0, j)),
            pl.BlockSpec((1, tn), lambda b_, i, j: (0, j)),
        ],
        out_specs=[
            pl.BlockSpec((1, tm, tn), lambda b_, i, j: (b_, i, j)),
            pl.BlockSpec((1, tm, 1), lambda b_, i, j: (b_, i, 0)),
            pl.BlockSpec((1, tm, 1), lambda b_, i, j: (b_, i, 0)),
        ],
        out_shape=[
            jax.ShapeDtypeStruct((B, S, N), BF16),
            jax.ShapeDtypeStruct((B, S, 1), F32),
            jax.ShapeDtypeStruct((B, S, 1), F32),
        ],
        scratch_shapes=[
            pltpu.VMEM((tm, D), BF16),
            pltpu.VMEM((tm, LANES), F32),
            pltpu.VMEM((tm, LANES), F32),
        ],
        compiler_params=_params("parallel", "parallel", "arbitrary"),
        name="sgu_in",
    )(x, gain, w, b)


def _sgu_out_kernel(u_ref, v_ref, mean_ref, rstd_ref, lng_ref, lnb_ref, ws_ref,
                    bs_ref, wo_prev_ref, wo_ref, wo_tail_ref, x_ref, o_ref, acc_ref, ga_ref,
                    gb_ref,
                    *, n_pairs):
    s = pl.program_id(2)
    gw = SGU_GW

    @pl.when(s == 0)
    def _():
        acc_ref[...] = jnp.zeros_like(acc_ref)
        gb_ref[...] = jnp.zeros_like(gb_ref)

    def gate_group(k, out_ref):
        g = 2 * s + k
        cols = slice(k * gw, (k + 1) * gw)
        vn = ((v_ref[0, :, cols].astype(F32) - mean_ref[0]) * rstd_ref[0] * lng_ref[g]
              + lnb_ref[g]).astype(BF16)
        ws = ws_ref[g]
        bs = bs_ref[g]
        for c in range(SGU_TM // SGU_CHUNK):
            rows = slice(c * SGU_CHUNK, (c + 1) * SGU_CHUNK)
            mixed = _dot(ws, vn[rows]) + bs
            out_ref[rows, :] = (u_ref[0, rows, cols].astype(F32) * mixed).astype(BF16)

    acc_ref[...] += _dot(gb_ref[...], wo_prev_ref[...])
    gate_group(0, ga_ref)
    acc_ref[...] += _dot(ga_ref[...], wo_ref[...])
    gate_group(1, gb_ref)

    @pl.when(s == n_pairs - 1)
    def _():
        o_ref[0] = x_ref[0] + acc_ref[...] + _dot(gb_ref[...], wo_tail_ref[...])


def _sgu_out(z, mean, rstd, ln_g, ln_b, w_s, b_s, w_out, x):
    B, S, D = x.shape
    tm = SGU_TM
    G = SGU_GROUPS
    gw = SGU_GW
    n_pairs = G // 2
    kern = functools.partial(_sgu_out_kernel, n_pairs=n_pairs)
    whole = lambda a: pl.BlockSpec(a.shape, lambda b, i, s: (0,) * a.ndim)
    return pl.pallas_call(
        kern,
        grid=(B, S // tm, n_pairs),
        in_specs=[
            pl.BlockSpec((1, tm, 2 * gw), lambda b, i, s: (b, i, s)),
            pl.BlockSpec((1, tm, 2 * gw), lambda b, i, s: (b, i, n_pairs + s)),
            pl.BlockSpec((1, tm, 1), lambda b, i, s: (b, i, 0)),
            pl.BlockSpec((1, tm, 1), lambda b, i, s: (b, i, 0)),
            whole(ln_g),
            whole(ln_b),
            whole(w_s),
            whole(b_s),
            pl.BlockSpec((gw, D), lambda b, i, s: (jnp.maximum(2 * s - 1, 0), 0)),
            pl.BlockSpec((gw, D), lambda b, i, s: (2 * s, 0)),
            pl.BlockSpec((gw, D), lambda b, i, s: (G - 1, 0)),
            pl.BlockSpec((1, tm, D), lambda b, i, s: (b, i, 0)),
        ],
        out_specs=pl.BlockSpec((1, tm, D), lambda b, i, s: (b, i, 0)),
        out_shape=jax.ShapeDtypeStruct((B, S, D), F32),
        scratch_shapes=[
            pltpu.VMEM((tm, D), F32),
            pltpu.VMEM((tm, gw), BF16),
            pltpu.VMEM((tm, gw), BF16),
        ],
        compiler_params=_params("parallel", "parallel", "arbitrary"),
        name="sgu_out",
    )(z, z, mean, rstd, ln_g, ln_b, w_s, b_s, w_out, w_out, w_out, x)


def _stage(a, scr_ref, first):
    tm = FFN_TM
    for c in range(FFN_HALF // LANES):
        blk = a[:, c * LANES:(c + 1) * LANES]
        scr_ref[first + c, 0:HALO, :] = blk[tm + HALO:tm + 2 * HALO]
        scr_ref[first + c, HALO:tm + 2 * HALO, :] = blk[:tm + HALO]


def _conv3(scr_ref, slab, cw, cb):
    tm = FFN_TM
    prev = scr_ref[slab, HALO - 1:HALO - 1 + tm, :]
    cur = scr_ref[slab, HALO:HALO + tm, :]
    nxt = scr_ref[slab, HALO + 1:HALO + 1 + tm, :]
    return prev * cw[0:1, :] + cur * cw[1:2, :] + nxt * cw[2:3, :] + cb


def _conv_ffn_kernel(x_hbm, xprev_ref, xnext_ref, gain_ref, wg_ref, wu_ref, cw_ref,
                     wo_ref, fin_ref, o_ref, xs_ref, xs_sem, hn_ref, acta_ref, actb_ref,
                     sa_ref, sb_ref, *, n_row_tiles, n_f, final_norm):
    b = pl.program_id(0)
    i = pl.program_id(1)
    j = pl.program_id(2)
    tm = FFN_TM
    nblk = FFN_HALF // LANES
    up0 = FFN_TF
    n_tiles = pl.num_programs(0) * n_row_tiles

    def x_copy(bb, ii):
        return pltpu.make_async_copy(x_hbm.at[bb, pl.ds(ii * tm, tm), :], xs_ref, xs_sem)

    @pl.when(j == 0)
    def _():
        @pl.when(jnp.logical_and(b == 0, i == 0))
        def _():
            x_copy(b, i).start()

        x_copy(b, i).wait()
        gain = gain_ref[...]
        hn_ref[0:tm, :] = _rms_rows(xs_ref[...], gain).astype(BF16)
        has_next = (i < n_row_tiles - 1).astype(F32)
        has_prev = (i > 0).astype(F32)
        halo = jnp.concatenate([_rms_rows(xnext_ref[0], gain) * has_next,
                                _rms_rows(xprev_ref[0], gain) * has_prev], axis=0)
        hn_ref[tm:tm + 2 * HALO, :] = halo.astype(BF16)
        o_ref[0] = xs_ref[...]

    @pl.when(j == 1)
    def _():
        nxt = b * n_row_tiles + i + 1

        @pl.when(nxt < n_tiles)
        def _():
            x_copy(nxt // n_row_tiles, nxt % n_row_tiles).start()

    def in_proj(lo, scr_ref):
        hn = hn_ref[...]
        _stage(_dot(hn, wg_ref[0, :, lo:lo + FFN_HALF]), scr_ref, 0)
        _stage(_dot(hn, wu_ref[0, :, lo:lo + FFN_HALF]), scr_ref, nblk)

    def conv_act(scr_ref, lo, act_ref):
        cw = cw_ref[j]
        for c in range(nblk):
            g0 = lo + c * LANES
            u0 = up0 + g0
            gate = _conv3(scr_ref, c, cw[0:3, g0:g0 + LANES], cw[3:4, g0:g0 + LANES])
            up = _conv3(scr_ref, nblk + c, cw[0:3, u0:u0 + LANES], cw[3:4, u0:u0 + LANES])
            act_ref[:, c * LANES:(c + 1) * LANES] = (_silu(gate) * up).astype(BF16)

    in_proj(0, sa_ref)
    in_proj(FFN_HALF, sb_ref)
    conv_act(sa_ref, 0, acta_ref)
    o_ref[0] += _dot(acta_ref[...], wo_ref[0, 0:FFN_HALF, :])
    conv_act(sb_ref, FFN_HALF, actb_ref)
    o_ref[0] += _dot(actb_ref[...], wo_ref[0, FFN_HALF:FFN_TF, :])

    if final_norm:
        @pl.when(j == n_f - 1)
        def _():
            o_ref[0] = _rms_rows(o_ref[0], fin_ref[...])


def _conv_ffn(x, gain, w_in, conv_tiles, w_out, fin_gain, *, layer, final_norm):
    B, S, D = x.shape
    tm, tf = FFN_TM, FFN_TF
    n_f = FFN_DIM // tf
    n_row_tiles = S // tm
    hb = tm // HALO
    last_hb = S // HALO - 1
    kern = functools.partial(_conv_ffn_kernel, n_row_tiles=n_row_tiles, n_f=n_f,
                             final_norm=final_norm)
    return pl.pallas_call(
        kern,
        grid=(B, n_row_tiles, n_f),
        in_specs=[
            pl.BlockSpec(memory_space=pl.ANY),
            pl.BlockSpec((1, HALO, D),
                         lambda b, i, j: (b, jnp.maximum(i * hb - 1, 0), 0)),
            pl.BlockSpec((1, HALO, D),
                         lambda b, i, j: (b, jnp.minimum((i + 1) * hb, last_hb), 0)),
            pl.BlockSpec((1, D), lambda b, i, j: (0, 0)),
            pl.BlockSpec((1, D, tf), lambda b, i, j: (layer, 0, j)),
            pl.BlockSpec((1, D, tf), lambda b, i, j: (layer, 0, n_f + j)),
            pl.BlockSpec(conv_tiles.shape, lambda b, i, j: (0, 0, 0)),
            pl.BlockSpec((1, tf, D), lambda b, i, j: (layer, j, 0)),
            pl.BlockSpec((1, D), lambda b, i, j: (0, 0)),
        ],
        out_specs=pl.BlockSpec((1, tm, D), lambda b, i, j: (b, i, 0)),
        out_shape=jax.ShapeDtypeStruct((B, S, D), F32),
        scratch_shapes=[
            pltpu.VMEM((tm, D), F32),
            pltpu.SemaphoreType.DMA(()),
            pltpu.VMEM((tm + 2 * HALO, D), BF16),
            pltpu.VMEM((tm, FFN_HALF), BF16),
            pltpu.VMEM((tm, FFN_HALF), BF16),
            pltpu.VMEM((2 * FFN_HALF // LANES, tm + 2 * HALO, LANES), F32),
            pltpu.VMEM((2 * FFN_HALF // LANES, tm + 2 * HALO, LANES), F32),
        ],
        compiler_params=_params("arbitrary", "arbitrary", "arbitrary"),
        name="conv_ffn",
    )(x, x, x, gain, w_in, w_in, conv_tiles, w_out, fin_gain)


def _conv_tiles(conv_w, conv_b):
    n_f = FFN_DIM // FFN_TF
    a = jnp.concatenate([conv_w, conv_b.reshape(1, -1)], axis=0)
    a = a.reshape(4, 2, n_f, FFN_TF)
    return a.transpose(2, 0, 1, 3).reshape(n_f, 4, 2 * FFN_TF)


def _rope_tables(S):
    half = RET_DK // 2
    inv = ROPE_BASE ** (-jnp.arange(half, dtype=F32) / half)
    ang = jnp.arange(S, dtype=F32)[:, None] * inv[None, :]
    return jnp.cos(ang), jnp.sin(ang)


def _row(v):
    return v.reshape(1, -1)


def _prepare(ln_mix, ln_ffn, ret_w_in, ret_log_decay_fwd, ret_log_decay_bwd,
             ret_gn_gain, ret_w_out, sgu_w_in, sgu_b_in, sgu_ln_g, sgu_ln_b,
             sgu_w_s, sgu_b_s, sgu_w_out, ffn_w_in, ffn_conv_w, ffn_conv_b,
             ffn_w_out, ln_final):
    depth = ln_mix.shape[0]
    ffn_w_in_all = ffn_w_in.astype(BF16)
    ffn_w_out_all = ffn_w_out.astype(BF16)
    layers = []
    for i in range(depth):
        j = i // 2
        p = {"ln_mix": _row(ln_mix[i]), "ln_ffn": _row(ln_ffn[i])}
        if i % 2 == 0:
            p["w_qkvg"] = ret_w_in[j].astype(BF16)
            p["lg_fwd"] = ret_log_decay_fwd[j]
            p["lg_bwd"] = ret_log_decay_bwd[j]
            p["gn_gain"] = _row(ret_gn_gain[j])
            p["w_out"] = ret_w_out[j].astype(BF16)
        else:
            p["w_in"] = sgu_w_in[j].astype(BF16)
            p["b_in"] = _row(sgu_b_in[j])
            p["ln_g"] = sgu_ln_g[j].reshape(SGU_GROUPS, 1, SGU_GW)
            p["ln_b"] = sgu_ln_b[j].reshape(SGU_GROUPS, 1, SGU_GW)
            p["w_s"] = sgu_w_s[j].astype(BF16)
            p["b_s"] = sgu_b_s[j][:, :, None]
            p["w_out"] = sgu_w_out[j].astype(BF16)
        p["ffn_w_in"] = ffn_w_in_all
        p["ffn_conv"] = _conv_tiles(ffn_conv_w[i], ffn_conv_b[i])
        p["ffn_w_out"] = ffn_w_out_all
        layers.append(p)
    return layers, _row(ln_final)


def _trunk(x, layers, fin_gain, cos, sin):
    for i, p in enumerate(layers):
        if i % 2 == 0:
            qkvg = _ret_proj(x, p["ln_mix"], p["w_qkvg"], cos, sin)
            s_rev = _ret_state(qkvg, p["lg_bwd"])
            a = _retention(qkvg, s_rev, p["lg_fwd"], p["lg_bwd"], p["gn_gain"])
            x = _out_proj(a, p["w_out"], x)
        else:
            z, mean, rstd = _sgu_in(x, p["ln_mix"], p["w_in"], p["b_in"])
            x = _sgu_out(z, mean, rstd, p["ln_g"], p["ln_b"], p["w_s"], p["b_s"],
                         p["w_out"], x)
        x = _conv_ffn(x, p["ln_ffn"], p["ffn_w_in"], p["ffn_conv"], p["ffn_w_out"],
                      fin_gain, layer=i, final_norm=(i == len(layers) - 1))
    return x


def kernel(x_prompt, x_sample, ln_mix, ln_ffn, ret_w_in, ret_log_decay_fwd, ret_log_decay_bwd, ret_gn_gain, ret_w_out, sgu_w_in, sgu_b_in, sgu_ln_g, sgu_ln_b, sgu_w_s, sgu_b_s, sgu_w_out, ffn_w_in, ffn_conv_w, ffn_conv_b, ffn_w_out, ln_final):
    layers, fin_gain = _prepare(
        ln_mix, ln_ffn, ret_w_in, ret_log_decay_fwd, ret_log_decay_bwd, ret_gn_gain,
        ret_w_out, sgu_w_in, sgu_b_in, sgu_ln_g, sgu_ln_b, sgu_w_s, sgu_b_s,
        sgu_w_out, ffn_w_in, ffn_conv_w, ffn_conv_b, ffn_w_out, ln_final)
    cos, sin = _rope_tables(max(x_prompt.shape[1], x_sample.shape[1]))
    return (_trunk(x_prompt, layers, fin_gain, cos, sin),
            _trunk(x_sample, layers, fin_gain, cos, sin))
```

```python
import functools

import jax
import jax.numpy as jnp
import numpy as np
from jax import lax
from jax.experimental import pallas as pl
from jax.experimental.pallas import tpu as pltpu

D_MODEL = 2048
RET_HEADS = 8
RET_DK = D_MODEL // RET_HEADS
RET_DV = 2 * D_MODEL // RET_HEADS
RET_HQ = RET_HEADS * RET_DK
RET_HV = RET_HEADS * RET_DV
RET_Q_BLK = 0
RET_K_BLK = 1
RET_V_BLK = 2 * RET_HQ // RET_HV
RET_G_BLK = RET_V_BLK + 1
ROPE_BASE = 10000.0
SGU_E = 3 * D_MODEL
SGU_GROUPS = 8
SGU_CHUNK = 128
SGU_GW = SGU_E // SGU_GROUPS
FFN_DIM = 5632
EPS = 1e-6
LN_EPS = 1e-5

SUBLANES_F32 = 8
LANES = 128
VMEM_LIMIT_BYTES = 60 * 1024 * 1024

PROJ_TM = 1024
PROJ_TN = 2048
PROJ_SUB = 1024
RET_BLOCK = 256
OUT_TM = 512
SGU_TM = 512
FFN_TM = 1024
FFN_TF = 512
FFN_HALF = FFN_TF // 2
HALO = SUBLANES_F32

BF16 = jnp.bfloat16
F32 = jnp.float32


def _params(*semantics):
    return pltpu.CompilerParams(dimension_semantics=semantics,
                                vmem_limit_bytes=VMEM_LIMIT_BYTES)


def _rms_rows(x, gain):
    ms = jnp.mean(x * x, axis=-1, keepdims=True)
    return x * lax.rsqrt(ms + EPS) * gain


def _dot(a, b):
    return jnp.dot(a, b, preferred_element_type=F32)


def _silu(a):
    return a * jax.nn.sigmoid(a)


def _ret_proj_kernel(x_ref, gain_ref, w_ref, cos_ref, sin_ref, o_ref, hn_ref,
                     *, n_query, n_rope, n_plain):
    j = pl.program_id(2)

    @pl.when(j == 0)
    def _():
        hn_ref[...] = _rms_rows(x_ref[0], gain_ref[...]).astype(BF16)

    def store_plain(acc, lo):
        o_ref[0, :, lo:lo + PROJ_SUB] = acc.astype(BF16)

    def store_silu(acc, lo):
        o_ref[0, :, lo:lo + PROJ_SUB] = _silu(acc).astype(BF16)

    def store_rope(acc, lo):
        cos = cos_ref[...]
        sin = sin_ref[...]
        half = RET_DK // 2
        scale = jnp.where(j >= n_query, RET_DK ** -0.5, 1.0).astype(F32)
        for hh in range(PROJ_SUB // RET_DK):
            a = hh * RET_DK
            x1 = acc[:, a:a + half]
            x2 = acc[:, a + half:a + RET_DK]
            o_ref[0, :, lo + a:lo + a + half] = ((x1 * cos - x2 * sin) * scale).astype(BF16)
            o_ref[0, :, lo + a + half:lo + a + RET_DK] = ((x2 * cos + x1 * sin) * scale).astype(BF16)

    def chains(store):
        for s in range(PROJ_TN // PROJ_SUB):
            lo = s * PROJ_SUB
            store(_dot(hn_ref[...], w_ref[:, lo:lo + PROJ_SUB]), lo)

    @pl.when(j < n_rope)
    def _():
        chains(store_rope)

    @pl.when(jnp.logical_and(j >= n_rope, j < n_rope + n_plain))
    def _():
        chains(store_plain)

    @pl.when(j >= n_rope + n_plain)
    def _():
        chains(store_silu)


def _ret_proj(x, gain, w_qkvg, cos, sin):
    B, S, D = x.shape
    N = w_qkvg.shape[1]
    tm, tn = PROJ_TM, PROJ_TN
    half = RET_DK // 2
    kern = functools.partial(_ret_proj_kernel, n_query=RET_HQ // tn, n_rope=2 * RET_HQ // tn,
                             n_plain=RET_HV // tn)
    return pl.pallas_call(
        kern,
        grid=(B, S // tm, N // tn),
        in_specs=[
            pl.BlockSpec((1, tm, D), lambda b, i, j: (b, i, 0)),
            pl.BlockSpec((1, D), lambda b, i, j: (0, 0)),
            pl.BlockSpec((D, tn), lambda b, i, j: (0, j)),
            pl.BlockSpec((tm, half), lambda b, i, j: (i, 0)),
            pl.BlockSpec((tm, half), lambda b, i, j: (i, 0)),
        ],
        out_specs=pl.BlockSpec((1, tm, tn), lambda b, i, j: (b, i, j)),
        out_shape=jax.ShapeDtypeStruct((B, S, N), BF16),
        scratch_shapes=[pltpu.VMEM((tm, D), BF16)],
        compiler_params=_params("parallel", "parallel", "arbitrary"),
        name="ret_proj",
    )(x, gain, w_qkvg, cos, sin)


_A_BT = (((1,), (1,)), ((), ()))
_AT_B = (((0,), (0,)), ((), ()))


def _ret_state_kernel(lg_ref, k_ref, v_ref, s_ref, state_ref, kd_ref, cd_ref):
    C = RET_BLOCK
    t = pl.program_id(1)

    @pl.when(t == 0)
    def _():
        state_ref[...] = jnp.zeros_like(state_ref)
        krow = lax.broadcasted_iota(jnp.int32, (C, RET_DK), 0).astype(F32)
        for h in range(RET_HEADS):
            lg = lg_ref[h]
            kd_ref[h] = jnp.exp(lg * krow)
            cd_ref[h] = jnp.exp(jnp.full((1, RET_DV), lg * C, F32))

    for h in range(RET_HEADS):
        kcols = slice(h * RET_DK, (h + 1) * RET_DK)
        vcols = slice(h * RET_DV, (h + 1) * RET_DV)
        s_ref[0, 0, kcols, :] = state_ref[h].astype(BF16)
        ks = (k_ref[0, :, kcols].astype(F32) * kd_ref[h]).astype(BF16)
        state_ref[h] = (state_ref[h] * cd_ref[h]
                        + lax.dot_general(ks, v_ref[0, :, vcols], _AT_B,
                                          preferred_element_type=F32))


def _ret_state(qkvg, log_gamma):
    B, S, _ = qkvg.shape
    C = RET_BLOCK
    H = RET_HEADS
    nC = S // C
    return pl.pallas_call(
        _ret_state_kernel,
        grid=(B, nC),
        in_specs=[
            pl.BlockSpec(memory_space=pltpu.SMEM),
            pl.BlockSpec((1, C, RET_HQ), lambda b, t: (b, nC - 1 - t, RET_K_BLK)),
            pl.BlockSpec((1, C, RET_HV), lambda b, t: (b, nC - 1 - t, RET_V_BLK)),
        ],
        out_specs=pl.BlockSpec((1, 1, RET_HQ, RET_DV), lambda b, t: (b, nC - 1 - t, 0, 0)),
        out_shape=jax.ShapeDtypeStruct((B, nC, RET_HQ, RET_DV), BF16),
        scratch_shapes=[
            pltpu.VMEM((H, RET_DK, RET_DV), F32),
            pltpu.VMEM((H, C, RET_DK), F32),
            pltpu.VMEM((H, 1, RET_DV), F32),
        ],
        compiler_params=_params("parallel", "arbitrary"),
        name="retention_state",
    )(log_gamma, qkvg, qkvg)


def _retention_kernel(lgf_ref, lgb_ref, q_ref, k_ref, v_ref, srev_ref, gate_ref, gain_ref,
                      o_ref, state_ref, dec_ref, qdf_ref, qdb_ref, kd_ref, cd_ref):
    C = RET_BLOCK
    t = pl.program_id(1)

    @pl.when(t == 0)
    def _():
        state_ref[...] = jnp.zeros_like(state_ref)
        ri = lax.broadcasted_iota(jnp.int32, (C, C), 0)
        ci = lax.broadcasted_iota(jnp.int32, (C, C), 1)
        behind = jnp.maximum(ri - ci, 0).astype(F32)
        ahead = jnp.maximum(ci - ri, 0).astype(F32)
        qrow = lax.broadcasted_iota(jnp.int32, (C, RET_DV), 0).astype(F32)
        krow = lax.broadcasted_iota(jnp.int32, (C, RET_DK), 0).astype(F32)
        for h in range(RET_HEADS):
            lgf = lgf_ref[h]
            lgb = lgb_ref[h]
            dec_ref[h] = jnp.where(ri >= ci, jnp.exp(lgf * behind), jnp.exp(lgb * ahead))
            qdf_ref[h] = jnp.exp(lgf * (qrow + 1.0))
            qdb_ref[h] = jnp.exp(lgb * (C - qrow))
            kd_ref[h] = jnp.exp(lgf * (C - 1.0 - krow))
            cd_ref[h] = jnp.exp(jnp.full((1, RET_DV), lgf * C, F32))

    for h in range(RET_HEADS):
        kcols = slice(h * RET_DK, (h + 1) * RET_DK)
        vcols = slice(h * RET_DV, (h + 1) * RET_DV)
        q = q_ref[0, :, kcols]
        k = k_ref[0, :, kcols]
        v = v_ref[0, :, vcols]
        scores = lax.dot_general(q, k, _A_BT, preferred_element_type=F32) * dec_ref[h]
        y = _dot(scores.astype(BF16), v)
        y = y + qdf_ref[h] * _dot(q, state_ref[h].astype(BF16))
        y = y + qdb_ref[h] * _dot(q, srev_ref[0, 0, kcols, :])
        ks = (k.astype(F32) * kd_ref[h]).astype(BF16)
        state_ref[h] = (state_ref[h] * cd_ref[h]
                        + lax.dot_general(ks, v, _AT_B, preferred_element_type=F32))
        mu = jnp.mean(y, axis=-1, keepdims=True)
        yc = y - mu
        var = jnp.mean(yc * yc, axis=-1, keepdims=True)
        yn = yc * lax.rsqrt(var + LN_EPS) * gain_ref[:, vcols]
        o_ref[0, :, vcols] = (gate_ref[0, :, vcols].astype(F32) * yn).astype(BF16)


def _retention(qkvg, s_rev, lg_fwd, lg_bwd, gn_gain):
    B, S, _ = qkvg.shape
    C = RET_BLOCK
    H = RET_HEADS
    smem = pl.BlockSpec(memory_space=pltpu.SMEM)
    return pl.pallas_call(
        _retention_kernel,
        grid=(B, S // C),
        in_specs=[
            smem,
            smem,
            pl.BlockSpec((1, C, RET_HQ), lambda b, t: (b, t, RET_Q_BLK)),
            pl.BlockSpec((1, C, RET_HQ), lambda b, t: (b, t, RET_K_BLK)),
            pl.BlockSpec((1, C, RET_HV), lambda b, t: (b, t, RET_V_BLK)),
            pl.BlockSpec((1, 1, RET_HQ, RET_DV), lambda b, t: (b, t, 0, 0)),
            pl.BlockSpec((1, C, RET_HV), lambda b, t: (b, t, RET_G_BLK)),
            pl.BlockSpec((1, RET_HV), lambda b, t: (0, 0)),
        ],
        out_specs=pl.BlockSpec((1, C, RET_HV), lambda b, t: (b, t, 0)),
        out_shape=jax.ShapeDtypeStruct((B, S, RET_HV), BF16),
        scratch_shapes=[
            pltpu.VMEM((H, RET_DK, RET_DV), F32),
            pltpu.VMEM((H, C, C), F32),
            pltpu.VMEM((H, C, RET_DV), F32),
            pltpu.VMEM((H, C, RET_DV), F32),
            pltpu.VMEM((H, C, RET_DK), F32),
            pltpu.VMEM((H, 1, RET_DV), F32),
        ],
        compiler_params=_params("parallel", "arbitrary"),
        name="retention",
    )(lg_fwd, lg_bwd, qkvg, qkvg, qkvg, s_rev, qkvg, gn_gain)


def _out_proj_kernel(a_ref, w_ref, x_ref, o_ref):
    N = w_ref.shape[1]
    for s in range(N // PROJ_SUB):
        cols = slice(s * PROJ_SUB, (s + 1) * PROJ_SUB)
        o_ref[0, :, cols] = x_ref[0, :, cols] + _dot(a_ref[0], w_ref[:, cols])


def _out_proj(a, w, x):
    B, S, K = a.shape
    N = w.shape[1]
    tm = OUT_TM
    return pl.pallas_call(
        _out_proj_kernel,
        grid=(B, S // tm),
        in_specs=[
            pl.BlockSpec((1, tm, K), lambda b, i: (b, i, 0)),
            pl.BlockSpec((K, N), lambda b, i: (0, 0), pipeline_mode=pl.Buffered(1)),
            pl.BlockSpec((1, tm, N), lambda b, i: (b, i, 0)),
        ],
        out_specs=pl.BlockSpec((1, tm, N), lambda b, i: (b, i, 0)),
        out_shape=jax.ShapeDtypeStruct((B, S, N), F32),
        compiler_params=_params("parallel", "parallel"),
        name="out_proj",
    )(a, w, x)


def _sgu_in_kernel(x_ref, gain_ref, w_ref, b_ref, z_ref, mean_ref, rstd_ref,
                   hn_ref, s1_ref, s2_ref, *, n_u, n_tiles):
    j = pl.program_id(2)

    @pl.when(j == 0)
    def _():
        hn_ref[...] = _rms_rows(x_ref[0], gain_ref[...]).astype(BF16)
        s1_ref[...] = jnp.zeros_like(s1_ref)
        s2_ref[...] = jnp.zeros_like(s2_ref)

    def chains(with_stats):
        for s in range(PROJ_TN // PROJ_SUB):
            cols = slice(s * PROJ_SUB, (s + 1) * PROJ_SUB)
            acc = _dot(hn_ref[...], w_ref[:, cols]) + b_ref[:, cols]
            z = 0.5 * acc * (1.0 + lax.erf(acc * np.float32(np.sqrt(0.5))))
            z_ref[0, :, cols] = z.astype(BF16)
            if with_stats:
                zz = z * z
                p1 = z[:, 0:LANES]
                p2 = zz[:, 0:LANES]
                for c in range(1, PROJ_SUB // LANES):
                    p1 = p1 + z[:, c * LANES:(c + 1) * LANES]
                    p2 = p2 + zz[:, c * LANES:(c + 1) * LANES]
                s1_ref[...] += p1
                s2_ref[...] += p2

    @pl.when(j < n_u)
    def _():
        chains(False)

    @pl.when(j >= n_u)
    def _():
        chains(True)

    @pl.when(j == n_tiles - 1)
    def _():
        mean = jnp.sum(s1_ref[...], axis=-1, keepdims=True) * (1.0 / SGU_E)
        var = jnp.sum(s2_ref[...], axis=-1, keepdims=True) * (1.0 / SGU_E) - mean * mean
        mean_ref[0] = mean
        rstd_ref[0] = lax.rsqrt(var + LN_EPS)


def _sgu_in(x, gain, w, b):
    B, S, D = x.shape
    N = w.shape[1]
    tm, tn = PROJ_TM, PROJ_TN
    n_tiles = N // tn
    kern = functools.partial(_sgu_in_kernel, n_u=SGU_E // tn, n_tiles=n_tiles)
    return pl.pallas_call(
        kern,
        grid=(B, S // tm, n_tiles),
        in_specs=[
            pl.BlockSpec((1, tm, D), lambda b_, i, j: (b_, i, 0)),
            pl.BlockSpec((1, D), lambda b_, i, j: (0, 0)),
            pl.BlockSpec((D, tn), lambda b_, i, j: (0, j)),
            pl.BlockSpec((1, tn), lambda b_, i, j: (0, j)),
        ],
        out_specs=[
            pl.BlockSpec((1, tm, tn), lambda b_, i, j: (b_, i, j)),
            pl.BlockSpec((1, tm, 1), lambda b_, i, j: (b_, i, 0)),
            pl.BlockSpec((1, tm, 1), lambda b_, i, j: (b_, i, 0)),
        ],
        out_shape=[
            jax.ShapeDtypeStruct((B, S, N), BF16),
            jax.ShapeDtypeStruct((B, S, 1), F32),
            jax.ShapeDtypeStruct((B, S, 1), F32),
        ],
        scratch_shapes=[
            pltpu.VMEM((tm, D), BF16),
            pltpu.VMEM((tm, LANES), F32),
            pltpu.VMEM((tm, LANES), F32),
        ],
        compiler_params=_params("parallel", "parallel", "arbitrary"),
        name="sgu_in",
    )(x, gain, w, b)


def _sgu_out_kernel(u_ref, v_ref, mean_ref, rstd_ref, lng_ref, lnb_ref, ws_ref,
                    bs_ref, wo_prev_ref, wo_ref, wo_tail_ref, x_ref, o_ref, acc_ref, ga_ref,
                    gb_ref,
                    *, n_pairs):
    s = pl.program_id(2)
    gw = SGU_GW

    @pl.when(s == 0)
    def _():
        acc_ref[...] = jnp.zeros_like(acc_ref)
        gb_ref[...] = jnp.zeros_like(gb_ref)

    def gate_group(k, out_ref):
        g = 2 * s + k
        cols = slice(k * gw, (k + 1) * gw)
        vn = ((v_ref[0, :, cols].astype(F32) - mean_ref[0]) * rstd_ref[0] * lng_ref[g]
              + lnb_ref[g]).astype(BF16)
        ws = ws_ref[g]
        bs = bs_ref[g]
        for c in range(SGU_TM // SGU_CHUNK):
            rows = slice(c * SGU_CHUNK, (c + 1) * SGU_CHUNK)
            mixed = _dot(ws, vn[rows]) + bs
            out_ref[rows, :] = (u_ref[0, rows, cols].astype(F32) * mixed).astype(BF16)

    acc_ref[...] += _dot(gb_ref[...], wo_prev_ref[...])
    gate_group(0, ga_ref)
    acc_ref[...] += _dot(ga_ref[...], wo_ref[...])
    gate_group(1, gb_ref)

    @pl.when(s == n_pairs - 1)
    def _():
        o_ref[0] = x_ref[0] + acc_ref[...] + _dot(gb_ref[...], wo_tail_ref[...])


def _sgu_out(z, mean, rstd, ln_g, ln_b, w_s, b_s, w_out, x):
    B, S, D = x.shape
    tm = SGU_TM
    G = SGU_GROUPS
    gw = SGU_GW
    n_pairs = G // 2
    kern = functools.partial(_sgu_out_kernel, n_pairs=n_pairs)
    whole = lambda a: pl.BlockSpec(a.shape, lambda b, i, s: (0,) * a.ndim)
    return pl.pallas_call(
        kern,
        grid=(B, S // tm, n_pairs),
        in_specs=[
            pl.BlockSpec((1, tm, 2 * gw), lambda b, i, s: (b, i, s)),
            pl.BlockSpec((1, tm, 2 * gw), lambda b, i, s: (b, i, n_pairs + s)),
            pl.BlockSpec((1, tm, 1), lambda b, i, s: (b, i, 0)),
            pl.BlockSpec((1, tm, 1), lambda b, i, s: (b, i, 0)),
            whole(ln_g),
            whole(ln_b),
            whole(w_s),
            whole(b_s),
            pl.BlockSpec((gw, D), lambda b, i, s: (jnp.maximum(2 * s - 1, 0), 0)),
            pl.BlockSpec((gw, D), lambda b, i, s: (2 * s, 0)),
            pl.BlockSpec((gw, D), lambda b, i, s: (G - 1, 0)),
            pl.BlockSpec((1, tm, D), lambda b, i, s: (b, i, 0)),
        ],
        out_specs=pl.BlockSpec((1, tm, D), lambda b, i, s: (b, i, 0)),
        out_shape=jax.ShapeDtypeStruct((B, S, D), F32),
        scratch_shapes=[
            pltpu.VMEM((tm, D), F32),
            pltpu.VMEM((tm, gw), BF16),
            pltpu.VMEM((tm, gw), BF16),
        ],
        compiler_params=_params("parallel", "parallel", "arbitrary"),
        name="sgu_out",
    )(z, z, mean, rstd, ln_g, ln_b, w_s, b_s, w_out, w_out, w_out, x)


def _stage(a, scr_ref, first):
    tm = FFN_TM
    for c in range(FFN_HALF // LANES):
        blk = a[:, c * LANES:(c + 1) * LANES]
        scr_ref[first + c, 0:HALO, :] = blk[tm + HALO:tm + 2 * HALO]
        scr_ref[first + c, HALO:tm + 2 * HALO, :] = blk[:tm + HALO]


def _conv3(scr_ref, slab, cw, cb):
    tm = FFN_TM
    prev = scr_ref[slab, HALO - 1:HALO - 1 + tm, :]
    cur = scr_ref[slab, HALO:HALO + tm, :]
    nxt = scr_ref[slab, HALO + 1:HALO + 1 + tm, :]
    return prev * cw[0:1, :] + cur * cw[1:2, :] + nxt * cw[2:3, :] + cb


def _conv_ffn_kernel(x_hbm, xprev_ref, xnext_ref, gain_ref, wg_ref, wu_ref, cw_ref,
                     wo_ref, fin_ref, o_ref, xs_ref, xs_sem, hn_ref, acta_ref, actb_ref,
                     sa_ref, sb_ref, *, n_row_tiles, n_f, final_norm):
    b = pl.program_id(0)
    i = pl.program_id(1)
    j = pl.program_id(2)
    tm = FFN_TM
    nblk = FFN_HALF // LANES
    up0 = FFN_TF
    n_tiles = pl.num_programs(0) * n_row_tiles

    def x_copy(bb, ii):
        return pltpu.make_async_copy(x_hbm.at[bb, pl.ds(ii * tm, tm), :], xs_ref, xs_sem)

    @pl.when(j == 0)
    def _():
        @pl.when(jnp.logical_and(b == 0, i == 0))
        def _():
            x_copy(b, i).start()

        x_copy(b, i).wait()
        gain = gain_ref[...]
        hn_ref[0:tm, :] = _rms_rows(xs_ref[...], gain).astype(BF16)
        has_next = (i < n_row_tiles - 1).astype(F32)
        has_prev = (i > 0).astype(F32)
        halo = jnp.concatenate([_rms_rows(xnext_ref[0], gain) * has_next,
                                _rms_rows(xprev_ref[0], gain) * has_prev], axis=0)
        hn_ref[tm:tm + 2 * HALO, :] = halo.astype(BF16)
        o_ref[0] = xs_ref[...]

    @pl.when(j == 1)
    def _():
        nxt = b * n_row_tiles + i + 1

        @pl.when(nxt < n_tiles)
        def _():
            x_copy(nxt // n_row_tiles, nxt % n_row_tiles).start()

    def in_proj(lo, scr_ref):
        hn = hn_ref[...]
        _stage(_dot(hn, wg_ref[0, :, lo:lo + FFN_HALF]), scr_ref, 0)
        _stage(_dot(hn, wu_ref[0, :, lo:lo + FFN_HALF]), scr_ref, nblk)

    def conv_act(scr_ref, lo, act_ref):
        cw = cw_ref[j]
        for c in range(nblk):
            g0 = lo + c * LANES
            u0 = up0 + g0
            gate = _conv3(scr_ref, c, cw[0:3, g0:g0 + LANES], cw[3:4, g0:g0 + LANES])
            up = _conv3(scr_ref, nblk + c, cw[0:3, u0:u0 + LANES], cw[3:4, u0:u0 + LANES])
            act_ref[:, c * LANES:(c + 1) * LANES] = (_silu(gate) * up).astype(BF16)

    in_proj(0, sa_ref)
    in_proj(FFN_HALF, sb_ref)
    conv_act(sa_ref, 0, acta_ref)
    o_ref[0] += _dot(acta_ref[...], wo_ref[0, 0:FFN_HALF, :])
    conv_act(sb_ref, FFN_HALF, actb_ref)
    o_ref[0] += _dot(actb_ref[...], wo_ref[0, FFN_HALF:FFN_TF, :])

    if final_norm:
        @pl.when(j == n_f - 1)
        def _():
            o_ref[0] = _rms_rows(o_ref[0], fin_ref[...])


def _conv_ffn(x, gain, w_in, conv_tiles, w_out, fin_gain, *, layer, final_norm):
    B, S, D = x.shape
    tm, tf = FFN_TM, FFN_TF
    n_f = FFN_DIM // tf
    n_row_tiles = S // tm
    hb = tm // HALO
    last_hb = S // HALO - 1
    kern = functools.partial(_conv_ffn_kernel, n_row_tiles=n_row_tiles, n_f=n_f,
                             final_norm=final_norm)
    return pl.pallas_call(
        kern,
        grid=(B, n_row_tiles, n_f),
        in_specs=[
            pl.BlockSpec(memory_space=pl.ANY),
            pl.BlockSpec((1, HALO, D),
                         lambda b, i, j: (b, jnp.maximum(i * hb - 1, 0), 0)),
            pl.BlockSpec((1, HALO, D),
                         lambda b, i, j: (b, jnp.minimum((i + 1) * hb, last_hb), 0)),
            pl.BlockSpec((1, D), lambda b, i, j: (0, 0)),
            pl.BlockSpec((1, D, tf), lambda b, i, j: (layer, 0, j)),
            pl.BlockSpec((1, D, tf), lambda b, i, j: (layer, 0, n_f + j)),
            pl.BlockSpec(conv_tiles.shape, lambda b, i, j: (0, 0, 0)),
            pl.BlockSpec((1, tf, D), lambda b, i, j: (layer, j, 0)),
            pl.BlockSpec((1, D), lambda b, i, j: (0, 0)),
        ],
        out_specs=pl.BlockSpec((1, tm, D), lambda b, i, j: (b, i, 0)),
        out_shape=jax.ShapeDtypeStruct((B, S, D), F32),
        scratch_shapes=[
            pltpu.VMEM((tm, D), F32),
            pltpu.SemaphoreType.DMA(()),
            pltpu.VMEM((tm + 2 * HALO, D), BF16),
            pltpu.VMEM((tm, FFN_HALF), BF16),
            pltpu.VMEM((tm, FFN_HALF), BF16),
            pltpu.VMEM((2 * FFN_HALF // LANES, tm + 2 * HALO, LANES), F32),
            pltpu.VMEM((2 * FFN_HALF // LANES, tm + 2 * HALO, LANES), F32),
        ],
        compiler_params=_params("arbitrary", "arbitrary", "arbitrary"),
        name="conv_ffn",
    )(x, x, x, gain, w_in, w_in, conv_tiles, w_out, fin_gain)


def _conv_tiles(conv_w, conv_b):
    n_f = FFN_DIM // FFN_TF
    a = jnp.concatenate([conv_w, conv_b.reshape(1, -1)], axis=0)
    a = a.reshape(4, 2, n_f, FFN_TF)
    return a.transpose(2, 0, 1, 3).reshape(n_f, 4, 2 * FFN_TF)


def _rope_tables(S):
    half = RET_DK // 2
    inv = ROPE_BASE ** (-jnp.arange(half, dtype=F32) / half)
    ang = jnp.arange(S, dtype=F32)[:, None] * inv[None, :]
    return jnp.cos(ang), jnp.sin(ang)


def _row(v):
    return v.reshape(1, -1)


def _prepare(ln_mix, ln_ffn, ret_w_in, ret_log_decay_fwd, ret_log_decay_bwd,
             ret_gn_gain, ret_w_out, sgu_w_in, sgu_b_in, sgu_ln_g, sgu_ln_b,
             sgu_w_s, sgu_b_s, sgu_w_out, ffn_w_in, ffn_conv_w, ffn_conv_b,
             ffn_w_out, ln_final):
    depth = ln_mix.shape[0]
    ffn_w_in_all = ffn_w_in.astype(BF16)
    ffn_w_out_all = ffn_w_out.astype(BF16)
    layers = []
    for i in range(depth):
        j = i // 2
        p = {"ln_mix": _row(ln_mix[i]), "ln_ffn": _row(ln_ffn[i])}
        if i % 2 == 0:
            p["w_qkvg"] = ret_w_in[j].astype(BF16)
            p["lg_fwd"] = ret_log_decay_fwd[j]
            p["lg_bwd"] = ret_log_decay_bwd[j]
            p["gn_gain"] = _row(ret_gn_gain[j])
            p["w_out"] = ret_w_out[j].astype(BF16)
        else:
            p["w_in"] = sgu_w_in[j].astype(BF16)
            p["b_in"] = _row(sgu_b_in[j])
            p["ln_g"] = sgu_ln_g[j].reshape(SGU_GROUPS, 1, SGU_GW)
            p["ln_b"] = sgu_ln_b[j].reshape(SGU_GROUPS, 1, SGU_GW)
            p["w_s"] = sgu_w_s[j].astype(BF16)
            p["b_s"] = sgu_b_s[j][:, :, None]
            p["w_out"] = sgu_w_out[j].astype(BF16)
        p["ffn_w_in"] = ffn_w_in_all
        p["ffn_conv"] = _conv_tiles(ffn_conv_w[i], ffn_conv_b[i])
        p["ffn_w_out"] = ffn_w_out_all
        layers.append(p)
    return layers, _row(ln_final)


def _trunk(x, layers, fin_gain, cos, sin):
    for i, p in enumerate(layers):
        if i % 2 == 0:
            qkvg = _ret_proj(x, p["ln_mix"], p["w_qkvg"], cos, sin)
            s_rev = _ret_state(qkvg, p["lg_bwd"])
            a = _retention(qkvg, s_rev, p["lg_fwd"], p["lg_bwd"], p["gn_gain"])
            x = _out_proj(a, p["w_out"], x)
        else:
            z, mean, rstd = _sgu_in(x, p["ln_mix"], p["w_in"], p["b_in"])
            x = _sgu_out(z, mean, rstd, p["ln_g"], p["ln_b"], p["w_s"], p["b_s"],
                         p["w_out"], x)
        x = _conv_ffn(x, p["ln_ffn"], p["ffn_w_in"], p["ffn_conv"], p["ffn_w_out"],
                      fin_gain, layer=i, final_norm=(i == len(layers) - 1))
    return x


def kernel(x_prompt, x_sample, ln_mix, ln_ffn, ret_w_in, ret_log_decay_fwd, ret_log_decay_bwd, ret_gn_gain, ret_w_out, sgu_w_in, sgu_b_in, sgu_ln_g, sgu_ln_b, sgu_w_s, sgu_b_s, sgu_w_out, ffn_w_in, ffn_conv_w, ffn_conv_b, ffn_w_out, ln_final):
    layers, fin_gain = _prepare(
        ln_mix, ln_ffn, ret_w_in, ret_log_decay_fwd, ret_log_decay_bwd, ret_gn_gain,
        ret_w_out, sgu_w_in, sgu_b_in, sgu_ln_g, sgu_ln_b, sgu_w_s, sgu_b_s,
        sgu_w_out, ffn_w_in, ffn_conv_w, ffn_conv_b, ffn_w_out, ln_final)
    cos, sin = _rope_tables(max(x_prompt.shape[1], x_sample.shape[1]))
    return (_trunk(x_prompt, layers, fin_gain, cos, sin),
            _trunk(x_sample, layers, fin_gain, cos, sin))
```
